```python
import math
import jax
import jax.numpy as jnp
from jax import lax
import numpy as np

D_MODEL = 4096
BATCH = 1
SEQ = 8192
DEPTH = 1

HEAD_DIM = 128
ATTN_PATTERNS = ((128, 1), (512, 4), (2048, 16))
N_ATTN_GROUPS = 3
ATTN_HEADS_PER_GROUP = 8
ATTN_WIDTH = N_ATTN_GROUPS * ATTN_HEADS_PER_GROUP * HEAD_DIM
ATTN_OUT_WIDTH = ATTN_HEADS_PER_GROUP * HEAD_DIM
DN_HEADS = 16
DN_WIDTH = DN_HEADS * HEAD_DIM
CONV_K = 4
DN_CHUNK = 64
D_FF = 4 * D_MODEL
PLE_DIM = 256
EPS = 1e-6
IN_SPLITS = (ATTN_WIDTH, ATTN_WIDTH, ATTN_WIDTH, 3 * DN_WIDTH, DN_WIDTH, DN_HEADS, DN_HEADS, D_MODEL, D_MODEL)
IN_WIDTH = sum(IN_SPLITS)

kernel_name = "hybrid_dilated_attn_gated_deltanet_block"


def _split_points(sizes):
    pts, acc = [], 0
    for n in sizes[:-1]:
        acc += n
        pts.append(acc)
    return pts


def rmsnorm(x, g):
    xf = x.astype(jnp.float32)
    y = xf * lax.rsqrt(jnp.mean(xf * xf, axis=-1, keepdims=True) + EPS)
    return (y * g.astype(jnp.float32)).astype(x.dtype)


def l2norm(t):
    return t * lax.rsqrt(jnp.sum(t * t, axis=-1, keepdims=True) + EPS)


def dilated_window_attention(q, k, v, window, dilation):
    b, s, g, hd = q.shape
    span = window // dilation
    L = s // dilation
    nb = -(-L // span)
    Lp = nb * span

    def to_sub(t):
        t = t.reshape(b, L, dilation, g, hd).transpose(0, 2, 1, 3, 4)
        t = jnp.pad(t, ((0, 0), (0, 0), (0, Lp - L), (0, 0), (0, 0)))
        return t.reshape(b, dilation, nb, span, g, hd)

    def with_prev(t):
        prev = jnp.pad(t, ((0, 0), (0, 0), (1, 0), (0, 0), (0, 0), (0, 0)))[:, :, :-1]
        return jnp.concatenate([prev, t], axis=3)

    qs = to_sub(q)
    kk = with_prev(to_sub(k))
    vv = with_prev(to_sub(v))
    scores = jnp.einsum('brnqhd,brnkhd->brnhqk', qs, kk).astype(jnp.float32) * (hd ** -0.5)
    qi = jnp.arange(span)[:, None]
    kj = jnp.arange(2 * span)[None, :]
    dist = qi + span - kj
    band = (dist >= 0) & (dist <= span)
    valid = band[None] & ((jnp.arange(nb)[:, None, None] > 0) | (kj[None] >= span))
    scores = jnp.where(valid[None, None, :, None], scores, -jnp.inf)
    m = jnp.max(scores, axis=-1, keepdims=True)
    e = jnp.exp(scores - m)
    l = jnp.sum(e, axis=-1, keepdims=True)
    o = jnp.einsum('brnhqk,brnkhd->brnqhd', (e / l).astype(v.dtype), vv)
    lse = (m + jnp.log(l))[..., 0]
    o = o.reshape(b, dilation, Lp, g, hd)[:, :, :L].transpose(0, 2, 1, 3, 4).reshape(b, s, g, hd)
    lse = lse.transpose(0, 1, 2, 4, 3).reshape(b, dilation, Lp, g)[:, :, :L]
    lse = lse.transpose(0, 2, 1, 3).reshape(b, s, g)
    return o, lse


def causal_conv_silu(x, w):
    kw = w.shape[0]
    s = x.shape[1]
    xp = jnp.pad(x, ((0, 0), (kw - 1, 0), (0, 0)))
    y = sum(xp[:, j:j + s] * w[j] for j in range(kw))
    return jax.nn.silu(y)


def chunked_gated_delta_rule(q, k, v, g, beta):
    b, s, h, dk = q.shape
    dv = v.shape[-1]
    c = DN_CHUNK
    nc = s // c
    q = q * dk ** -0.5

    def chunks(t):
        t = jnp.moveaxis(t, 2, 1)
        return t.reshape((b, h, nc, c) + t.shape[3:])

    qc, kc, vc = chunks(q), chunks(k), chunks(v)
    gc = jnp.cumsum(chunks(g), axis=-1)
    bc = chunks(beta)
    lower = jnp.tril(jnp.ones((c, c), dtype=bool))
    strict = jnp.tril(jnp.ones((c, c), dtype=bool), -1)
    decay = jnp.exp(jnp.where(lower, gc[..., :, None] - gc[..., None, :], -jnp.inf))
    kb = kc * bc[..., None]
    a = jnp.where(strict, jnp.einsum('bhnid,bhnjd->bhnij', kb, kc) * decay, 0.0)
    eye = jnp.eye(c, dtype=a.dtype)
    t_inv = lax.linalg.triangular_solve(eye + a, jnp.broadcast_to(eye, a.shape),
                                        left_side=True, lower=True, unit_diagonal=True)
    u = jnp.einsum('bhnij,bhnjd->bhnid', t_inv, vc * bc[..., None])
    w = jnp.einsum('bhnij,bhnjd->bhnid', t_inv, kb * jnp.exp(gc)[..., None])
    intra = jnp.where(lower, jnp.einsum('bhnid,bhnjd->bhnij', qc, kc) * decay, 0.0)
    g_last = gc[..., -1]
    k_dec = kc * jnp.exp(g_last[..., None] - gc)[..., None]
    q_dec = qc * jnp.exp(gc)[..., None]

    def step(state, xs):
        q_i, k_i, u_i, w_i, a_i, gl_i = xs
        v_new = u_i - jnp.einsum('bhcd,bhde->bhce', w_i, state)
        o_i = jnp.einsum('bhcd,bhde->bhce', q_i, state) + jnp.einsum('bhij,bhje->bhie', a_i, v_new)
        state = state * jnp.exp(gl_i)[..., None, None] + jnp.einsum('bhcd,bhce->bhde', k_i, v_new)
        return state, o_i

    xs = tuple(jnp.moveaxis(t, 2, 0) for t in (q_dec, k_dec, u, w, intra, g_last))
    state0 = jnp.zeros((b, h, dk, dv), jnp.float32)
    _, o = lax.scan(step, state0, xs)
    o = jnp.moveaxis(o, 0, 2).reshape(b, h, s, dv)
    return jnp.moveaxis(o, 1, 2)


def setup_inputs(seed: int = 0) -> dict:
    key = jax.random.key(seed)
    ks = jax.random.split(key, 20)
    f32 = jnp.float32

    def nrm(kk, shape, fan_in):
        return jax.random.normal(kk, shape, f32) * fan_in ** -0.5

    def gain(kk, shape):
        return 1.0 + 0.02 * jax.random.normal(kk, shape, f32)

    dt = jnp.exp(jax.random.uniform(ks[5], (DEPTH, DN_HEADS), f32, math.log(1e-3), math.log(1e-1)))
    return {
        "x": jax.random.normal(ks[0], (BATCH, SEQ, D_MODEL), f32),
        "p": jax.random.normal(ks[1], (DEPTH, BATCH, SEQ, PLE_DIM), f32),
        "w_in": nrm(ks[2], (DEPTH, D_MODEL, IN_WIDTH), D_MODEL),
        "conv_w": nrm(ks[3], (DEPTH, CONV_K, 3 * DN_WIDTH), CONV_K),
        "dn_a_log": jnp.log(jax.random.uniform(ks[4], (DEPTH, DN_HEADS), f32, 1.0, 16.0)),
        "dn_dt_bias": dt + jnp.log(-jnp.expm1(-dt)),
        "dn_norm": gain(ks[6], (DEPTH, HEAD_DIM)),
        "w_attn_up": nrm(ks[7], (DEPTH, ATTN_OUT_WIDTH, D_MODEL), ATTN_OUT_WIDTH),
        "w_dn_up": nrm(ks[8], (DEPTH, DN_WIDTH, D_MODEL), DN_WIDTH),
        "w_out": nrm(ks[9], (DEPTH, D_MODEL, D_MODEL), D_MODEL),
        "w_mlp_up": nrm(ks[10], (DEPTH, D_MODEL, D_FF), D_MODEL),
        "w_mlp_down": nrm(ks[11], (DEPTH, D_FF, D_MODEL), D_FF),
        "w_ple_gate": nrm(ks[12], (DEPTH, D_MODEL, D_MODEL), D_MODEL),
        "w_ple_proj": nrm(ks[13], (DEPTH, PLE_DIM, D_MODEL), PLE_DIM),
        "norm_mix": gain(ks[14], (DEPTH, D_MODEL)),
        "norm_mlp": gain(ks[15], (DEPTH, D_MODEL)),
        "norm_ple": gain(ks[16], (DEPTH, D_MODEL)),
        "ple_post_norm": gain(ks[17], (DEPTH, D_MODEL)),
        "final_norm": gain(ks[18], (D_MODEL,)),
    }


def reference(x, p, w_in, conv_w, dn_a_log, dn_dt_bias, dn_norm, w_attn_up, w_dn_up, w_out,
              w_mlp_up, w_mlp_down, w_ple_gate, w_ple_proj, norm_mix, norm_mlp, norm_ple,
              ple_post_norm, final_norm):
    b, s, _ = x.shape
    f32 = jnp.float32
    for i in range(DEPTH):
        h = rmsnorm(x, norm_mix[i])
        q_a, k_a, v_a, qkv_b, z_b, beta_raw, alpha_raw, gate_a, gate_b = jnp.split(
            h @ w_in[i], _split_points(IN_SPLITS), axis=-1)

        def grp(t):
            return t.reshape(b, s, N_ATTN_GROUPS, ATTN_HEADS_PER_GROUP, HEAD_DIM)
        q_a, k_a, v_a = grp(q_a), grp(k_a), grp(v_a)
        outs, lses = [], []
        for gi, (window, dilation) in enumerate(ATTN_PATTERNS):
            o_g, lse_g = dilated_window_attention(q_a[:, :, gi], k_a[:, :, gi], v_a[:, :, gi], window, dilation)
            outs.append(o_g)
            lses.append(lse_g)
        wts = jax.nn.softmax(jnp.stack(lses, axis=0), axis=0)
        o_a = jnp.sum(wts[..., None] * jnp.stack(outs, axis=0).astype(f32), axis=0)
        o_a = o_a.reshape(b, s, ATTN_OUT_WIDTH).astype(x.dtype)

        qkv_b = causal_conv_silu(qkv_b, conv_w[i]).astype(f32)
        q_b, k_b, v_b = [t.reshape(b, s, DN_HEADS, HEAD_DIM) for t in jnp.split(qkv_b, 3, axis=-1)]
        q_b, k_b = l2norm(q_b), l2norm(k_b)
        beta = jax.nn.sigmoid(beta_raw.astype(f32))
        g = -jnp.exp(dn_a_log[i].astype(f32)) * jax.nn.softplus(alpha_raw.astype(f32) + dn_dt_bias[i].astype(f32))
        o_b = chunked_gated_delta_rule(q_b, k_b, v_b, g, beta)
        o_b = rmsnorm(o_b, dn_norm[i]) * jax.nn.silu(z_b.astype(f32).reshape(b, s, DN_HEADS, HEAD_DIM))
        o_b = o_b.reshape(b, s, DN_WIDTH).astype(x.dtype)

        merged = jax.nn.sigmoid(gate_a) * (o_a @ w_attn_up[i]) + jax.nn.sigmoid(gate_b) * (o_b @ w_dn_up[i])
        x = x + merged @ w_out[i]

        h = rmsnorm(x, norm_mlp[i])
        x = x + jnp.square(jax.nn.relu(h @ w_mlp_up[i])) @ w_mlp_down[i]

        gate = jax.nn.sigmoid(rmsnorm(x, norm_ple[i]) @ w_ple_gate[i])
        x = x + gate * rmsnorm(p[i] @ w_ple_proj[i], ple_post_norm[i])
    return rmsnorm(x, final_norm)
```

```python
import functools
import math

import jax
import jax.numpy as jnp
from jax import lax
from jax.experimental import pallas as pl
from jax.experimental.pallas import tpu as pltpu

F32 = jnp.float32
BF16 = jnp.bfloat16

EPS = 1e-6
HEAD_DIM = 128
ATTN_SPAN = 128
ATTN_DILATIONS = (1, 4, 16)
ATTN_HEADS = 8
DN_HEADS = 16
DN_CHUNK = 64
CONV_K = 4

VMEM_LIMIT_BYTES = 56 * 1024 * 1024


def _params(n_axes):
    return pltpu.CompilerParams(
        dimension_semantics=("arbitrary",) * n_axes,
        vmem_limit_bytes=VMEM_LIMIT_BYTES,
    )


def _rmsnorm_kernel(x_ref, g_ref, o_ref):
    x = x_ref[...].astype(F32)
    ms = jnp.mean(x * x, axis=-1, keepdims=True)
    o_ref[...] = (x * lax.rsqrt(ms + EPS) * g_ref[...]).astype(o_ref.dtype)


def rmsnorm(x, g, out_dtype, *, tm=256):
    m, d = x.shape
    return pl.pallas_call(
        _rmsnorm_kernel,
        grid=(m // tm,),
        in_specs=[pl.BlockSpec((tm, d), lambda i: (i, 0)),
                  pl.BlockSpec((1, d), lambda i: (0, 0))],
        out_specs=pl.BlockSpec((tm, d), lambda i: (i, 0)),
        out_shape=jax.ShapeDtypeStruct((m, d), out_dtype),
        compiler_params=_params(1),
        name="rmsnorm",
    )(x, g.reshape(1, d).astype(F32))


def _sigmoid(x):
    return 1.0 / (1.0 + jnp.exp(-x))


def _mm_kernel(*refs, nk, tm, epilogue, n_extra):
    a_ref, b_ref = refs[0], refs[1]
    extra = refs[2:2 + n_extra]
    o_ref = refs[2 + n_extra]
    bcast_ref = refs[3 + n_extra]
    acc_ref = refs[4 + n_extra] if nk > 1 else None
    k = pl.program_id(1)
    i = pl.program_id(2)

    @pl.when(i == 0)
    def _():
        bcast_ref[...] = b_ref[...].astype(BF16)

    part = jnp.dot(a_ref[...], bcast_ref[...], preferred_element_type=F32)

    def finish(acc):
        if epilogue == "plain":
            val = acc
        elif epilogue == "relu2":
            r = jnp.maximum(acc, 0.0)
            val = r * r
        elif epilogue == "residual":
            val = extra[0][...].astype(F32) + acc
        elif epilogue == "gate":
            val = _sigmoid(extra[0][...].astype(F32)) * acc
        elif epilogue == "gate_add":
            val = extra[1][...].astype(F32) + _sigmoid(extra[0][...].astype(F32)) * acc
        elif epilogue == "ple":
            val = extra[0][...].astype(F32) + _sigmoid(acc) * extra[1][...].astype(F32)
        else:
            raise ValueError(epilogue)
        o_ref[...] = val.astype(o_ref.dtype)

    if nk == 1:
        finish(part)
    else:
        rows = pl.ds(pl.multiple_of(i * tm, tm), tm)

        @pl.when(k == 0)
        def _():
            acc_ref[rows, :] = part

        @pl.when(jnp.logical_and(k > 0, k < nk - 1))
        def _():
            acc_ref[rows, :] += part

        @pl.when(k == nk - 1)
        def _():
            finish(acc_ref[rows, :] + part)


def matmul_ws(a, b, *, col0=0, n=None, out_dtype=BF16, epilogue="plain", extra=(),
              tm=1024, tn=512, tk=None, name="matmul_ws"):
    m, kdim = a.shape
    if n is None:
        n = b.shape[1] - col0
    if tk is None:
        tk = kdim
    assert m % tm == 0 and n % tn == 0 and kdim % tk == 0 and col0 % tn == 0
    nj, nk, ni = n // tn, kdim // tk, m // tm
    j0 = col0 // tn

    def out_map(j, k, i):
        return (jnp.where(k == nk - 1, i, 0), j)

    in_specs = [pl.BlockSpec((tm, tk), lambda j, k, i: (i, k)),
                pl.BlockSpec((tk, tn), lambda j, k, i: (k, j + j0))]
    for _, ecol0 in extra:
        assert ecol0 % tn == 0
        in_specs.append(pl.BlockSpec(
            (tm, tn), functools.partial(lambda j, k, i, e0: (out_map(j, k, i)[0], j + e0),
                                        e0=ecol0 // tn)))
    scratch = [pltpu.VMEM((tk, tn), BF16)]
    if nk > 1:
        scratch.append(pltpu.VMEM((m, tn), F32))
    return pl.pallas_call(
        functools.partial(_mm_kernel, nk=nk, tm=tm, epilogue=epilogue, n_extra=len(extra)),
        grid=(nj, nk, ni),
        in_specs=in_specs,
        out_specs=pl.BlockSpec((tm, tn), out_map),
        out_shape=jax.ShapeDtypeStruct((m, n), out_dtype),
        scratch_shapes=scratch,
        compiler_params=_params(3),
        name=name,
    )(a, b, *[arr for arr, _ in extra])


def _attn_kernel(q_ref, kp_ref, kc_ref, vp_ref, vc_ref, o_ref, lse_ref, *, heads, scale):
    n = pl.program_id(1)
    span = q_ref.shape[0]
    qi = lax.broadcasted_iota(jnp.int32, (span, 2 * span), 0)
    kj = lax.broadcasted_iota(jnp.int32, (span, 2 * span), 1)
    dist = qi + span - kj
    valid = (dist >= 0) & (dist <= span) & ((kj >= span) | (n > 0))
    lane = lax.broadcasted_iota(jnp.int32, (span, HEAD_DIM), 1)
    lse_tile = jnp.zeros((span, HEAD_DIM), F32)
    for h in range(heads):
        cols = slice(h * HEAD_DIM, (h + 1) * HEAD_DIM)
        q = q_ref[:, cols]
        k = jnp.concatenate([kp_ref[:, cols], kc_ref[:, cols]], axis=0)
        v = jnp.concatenate([vp_ref[:, cols], vc_ref[:, cols]], axis=0)
        s = lax.dot_general(q, k, (((1,), (1,)), ((), ())), preferred_element_type=F32) * scale
        s = jnp.where(valid, s, -jnp.inf)
        m = jnp.max(s, axis=-1, keepdims=True)
        e = jnp.exp(s - m)
        l = jnp.sum(e, axis=-1, keepdims=True)
        o = jnp.dot(e.astype(BF16), v, preferred_element_type=F32) / l
        o_ref[:, cols] = o.astype(o_ref.dtype)
        lse_tile = jnp.where(lane == h, m + jnp.log(l), lse_tile)
    lse_ref[...] = lse_tile


def banded_attention(q, k, v, *, heads=ATTN_HEADS, col_blocks=(0, 0, 0)):
    d, length, _ = q.shape
    width = heads * HEAD_DIM
    span = ATTN_SPAN
    nb = length // span

    def cur_spec(cb):
        return pl.BlockSpec((None, span, width), lambda r, n: (r, n, cb))

    def prev_spec(cb):
        return pl.BlockSpec((None, span, width), lambda r, n: (r, jnp.maximum(n - 1, 0), cb))

    qc, kc, vc = col_blocks
    return pl.pallas_call(
        functools.partial(_attn_kernel, heads=heads, scale=HEAD_DIM ** -0.5),
        grid=(d, nb),
        in_specs=[cur_spec(qc), prev_spec(kc), cur_spec(kc), prev_spec(vc), cur_spec(vc)],
        out_specs=[pl.BlockSpec((None, span, width), lambda r, n: (r, n, 0)),
                   pl.BlockSpec((None, span, HEAD_DIM), lambda r, n: (r, n, 0))],
        out_shape=[jax.ShapeDtypeStruct((d, length, width), F32),
                   jax.ShapeDtypeStruct((d, length, HEAD_DIM), F32)],
        compiler_params=_params(2),
        name="banded_attention",
    )(q, k, k, v, v)


def _attn_combine_kernel(o0_ref, o1_ref, o2_ref, l0_ref, l1_ref, l2_ref, out_ref, *, heads):
    l0, l1, l2 = l0_ref[...], l1_ref[...], l2_ref[...]
    m = jnp.maximum(jnp.maximum(l0, l1), l2)
    e0, e1, e2 = jnp.exp(l0 - m), jnp.exp(l1 - m), jnp.exp(l2 - m)
    den = e0 + e1 + e2
    w0, w1, w2 = e0 / den, e1 / den, e2 / den
    for h in range(heads):
        cols = slice(h * HEAD_DIM, (h + 1) * HEAD_DIM)
        acc = (w0[:, h:h + 1] * o0_ref[:, cols] + w1[:, h:h + 1] * o1_ref[:, cols]
               + w2[:, h:h + 1] * o2_ref[:, cols])
        out_ref[:, cols] = acc.astype(out_ref.dtype)


def attn_combine(outs, lses, *, tm=256):
    m, width = outs[0].shape
    heads = width // HEAD_DIM
    ospec = pl.BlockSpec((tm, width), lambda i: (i, 0))
    lspec = pl.BlockSpec((tm, HEAD_DIM), lambda i: (i, 0))
    return pl.pallas_call(
        functools.partial(_attn_combine_kernel, heads=heads),
        grid=(m // tm,),
        in_specs=[ospec] * 3 + [lspec] * 3,
        out_specs=ospec,
        out_shape=jax.ShapeDtypeStruct((m, width), BF16),
        compiler_params=_params(1),
        name="attn_combine",
    )(*outs, *lses)


def _softplus(x):
    return jnp.maximum(x, 0.0) + jnp.log(1.0 + jnp.exp(-jnp.abs(x)))


def _dot_hi(a, b):
    return jnp.dot(a, b, preferred_element_type=F32, precision=lax.Precision.HIGHEST)


def _unit_lower_inverse(a):
    c = a.shape[0]
    eye = (lax.broadcasted_iota(jnp.int32, (c, c), 0)
           == lax.broadcasted_iota(jnp.int32, (c, c), 1)).astype(F32)
    p = eye - a
    x = a
    for _ in range(int(math.log2(c)) - 1):
        x = _dot_hi(x, x)
        p = p + _dot_hi(p, x)
    return p


def _deltanet_kernel(qp_ref, qh_ref, kp_ref, kh_ref, vp_ref, vh_ref, cwq_ref, cwk_ref, cwv_ref,
                     ba_ref, bat_ref, prow_ref, pcol_ref, z_ref, gn_ref, o_ref, state_ref,
                     *, heads, chunks):
    tb = pl.program_id(0)
    c_len = DN_CHUNK
    rows_total = chunks * c_len

    @pl.when(tb == 0)
    def _():
        state_ref[...] = jnp.zeros_like(state_ref)

    def conv_silu(x_ref, halo_ref, w_ref):
        halo = jnp.where(tb == 0, 0.0, halo_ref[...])
        xx = jnp.concatenate([halo, x_ref[...]], axis=0)
        w = w_ref[...]
        base = halo.shape[0] - (CONV_K - 1)
        y = xx[base:base + rows_total] * w[0:1]
        for j in range(1, CONV_K):
            y = y + xx[base + j:base + j + rows_total] * w[j:j + 1]
        return y * _sigmoid(y)

    qc = conv_silu(qp_ref, qh_ref, cwq_ref)
    kc = conv_silu(kp_ref, kh_ref, cwk_ref)
    vc = conv_silu(vp_ref, vh_ref, cwv_ref)

    ba = ba_ref[...]
    beta_col = _sigmoid(ba)
    g_col = -jnp.exp(prow_ref[0:1, :]) * _softplus(ba + prow_ref[1:2, :])
    row_in_chunk = lax.broadcasted_iota(jnp.int32, g_col.shape, 0) % c_len
    shift = 1
    while shift < c_len:
        g_col = g_col + jnp.where(row_in_chunk >= shift, pltpu.roll(g_col, shift, 0), 0.0)
        shift *= 2
    bat = bat_ref[...]
    beta_row = _sigmoid(bat)
    g_row = -jnp.exp(pcol_ref[:, 0:1])[None] * _softplus(bat + pcol_ref[:, 1:2][None])
    lane_in_chunk = lax.broadcasted_iota(jnp.int32, g_row.shape, 2)
    shift = 1
    while shift < c_len:
        g_row = g_row + jnp.where(lane_in_chunk >= shift, pltpu.roll(g_row, shift, 2), 0.0)
        shift *= 2

    ri = lax.broadcasted_iota(jnp.int32, (c_len, c_len), 0)
    ci = lax.broadcasted_iota(jnp.int32, (c_len, c_len), 1)
    lower = ri >= ci
    strict = ri > ci
    gn = gn_ref[...]

    for c in range(chunks):
        rs = slice(c * c_len, (c + 1) * c_len)
        for h in range(heads):
            cs = slice(h * HEAD_DIM, (h + 1) * HEAD_DIM)
            q = qc[rs, cs]
            k = kc[rs, cs]
            v = vc[rs, cs]
            q = q * (lax.rsqrt(jnp.sum(q * q, axis=-1, keepdims=True) + EPS) * HEAD_DIM ** -0.5)
            k = k * lax.rsqrt(jnp.sum(k * k, axis=-1, keepdims=True) + EPS)
            gcol = g_col[rs, heads + h:heads + h + 1]
            bcol = beta_col[rs, h:h + 1]
            grow = g_row[c, heads + h:heads + h + 1, :]
            brow = beta_row[c, h:h + 1, :]
            decay = jnp.exp(jnp.where(lower, gcol - grow, -jnp.inf))
            kb16 = k.astype(BF16)
            qk = lax.dot_general(jnp.concatenate([q.astype(BF16), kb16], axis=0), kb16,
                                 (((1,), (1,)), ((), ())), preferred_element_type=F32)
            intra = qk[:c_len] * decay
            a = jnp.where(strict, qk[c_len:] * decay * bcol, 0.0)
            t_inv = _unit_lower_inverse(a)
            u = jnp.dot((t_inv * brow).astype(BF16), v.astype(BF16), preferred_element_type=F32)
            w = jnp.dot((t_inv * (brow * jnp.exp(grow))).astype(BF16), kb16,
                        preferred_element_type=F32)
            s = state_ref[h]
            s16 = s.astype(BF16)
            v_new = u - jnp.dot(w.astype(BF16), s16, preferred_element_type=F32)
            o = (jnp.dot(q.astype(BF16), s16, preferred_element_type=F32) * jnp.exp(gcol)
                 + jnp.dot(intra.astype(BF16), v_new.astype(BF16), preferred_element_type=F32))
            g_last = grow[:, c_len - 1:c_len]
            k_dec = k * jnp.exp(g_last - gcol)
            state_ref[h] = s * jnp.exp(g_last) + lax.dot_general(
                k_dec.astype(BF16), v_new.astype(BF16), (((0,), (0,)), ((), ())),
                preferred_element_type=F32)
            o = o * lax.rsqrt(jnp.mean(o * o, axis=-1, keepdims=True) + EPS) * gn
            zz = z_ref[rs, cs].astype(F32)
            o_ref[rs, cs] = (o * (zz * _sigmoid(zz))).astype(o_ref.dtype)


def gated_deltanet(qkv_pre, conv_w, ba, a_log, dt_bias, z, dn_norm, *, chunks=2):
    s_len, width = z.shape
    heads = width // HEAD_DIM
    c_len = DN_CHUNK
    tb_rows = chunks * c_len
    halo = 8
    n_blocks = s_len // tb_rows
    bat = ba[:, :2 * heads].reshape(s_len // c_len, c_len, 2 * heads).transpose(0, 2, 1)
    prow = jnp.zeros((2, HEAD_DIM), F32)
    prow = prow.at[0, heads:2 * heads].set(a_log.astype(F32))
    prow = prow.at[1, heads:2 * heads].set(dt_bias.astype(F32))
    pcol = jnp.zeros((2 * heads, 2), F32)
    pcol = pcol.at[heads:, 0].set(a_log.astype(F32)).at[heads:, 1].set(dt_bias.astype(F32))
    main = pl.BlockSpec((tb_rows, width), lambda t: (t, 0))

    def main_spec(i):
        return pl.BlockSpec((tb_rows, width), lambda t: (t, i))

    def halo_spec(i):
        return pl.BlockSpec((halo, width),
                            lambda t: (jnp.maximum(t * (tb_rows // halo) - 1, 0), i))

    cw_specs = [pl.BlockSpec((CONV_K, width), functools.partial(lambda t, i: (0, i), i=i))
                for i in range(3)]
    return pl.pallas_call(
        functools.partial(_deltanet_kernel, heads=heads, chunks=chunks),
        grid=(n_blocks,),
        in_specs=[main_spec(0), halo_spec(0), main_spec(1), halo_spec(1), main_spec(2),
                  halo_spec(2), *cw_specs,
                  pl.BlockSpec((tb_rows, HEAD_DIM), lambda t: (t, 0)),
                  pl.BlockSpec((chunks, 2 * heads, c_len), lambda t: (t, 0, 0)),
                  pl.BlockSpec((2, HEAD_DIM), lambda t: (0, 0)),
                  pl.BlockSpec((2 * heads, 2), lambda t: (0, 0)),
                  main,
                  pl.BlockSpec((1, HEAD_DIM), lambda t: (0, 0))],
        out_specs=main,
        out_shape=jax.ShapeDtypeStruct((s_len, width), BF16),
        scratch_shapes=[pltpu.VMEM((heads, HEAD_DIM, HEAD_DIM), F32)],
        compiler_params=_params(1),
        name="gated_deltanet",
    )(qkv_pre, qkv_pre, qkv_pre, qkv_pre, qkv_pre, qkv_pre, conv_w, conv_w, conv_w, ba, bat,
      prow, pcol, z, dn_norm.reshape(1, HEAD_DIM).astype(F32))


def _proj_norm_kernel(p_ref, w_ref, g_ref, o_ref, wcast_ref):
    @pl.when(pl.program_id(0) == 0)
    def _():
        wcast_ref[...] = w_ref[...].astype(BF16)

    y = jnp.dot(p_ref[...].astype(BF16), wcast_ref[...], preferred_element_type=F32)
    ms = jnp.mean(y * y, axis=-1, keepdims=True)
    o_ref[...] = (y * lax.rsqrt(ms + EPS) * g_ref[...]).astype(o_ref.dtype)


def proj_norm(p, w, g, out_dtype, *, tm=512):
    m, kdim = p.shape
    n = w.shape[1]
    return pl.pallas_call(
        _proj_norm_kernel,
        grid=(m // tm,),
        in_specs=[pl.BlockSpec((tm, kdim), lambda i: (i, 0)),
                  pl.BlockSpec((kdim, n), lambda i: (0, 0)),
                  pl.BlockSpec((1, n), lambda i: (0, 0))],
        out_specs=pl.BlockSpec((tm, n), lambda i: (i, 0)),
        out_shape=jax.ShapeDtypeStruct((m, n), out_dtype),
        scratch_shapes=[pltpu.VMEM((kdim, n), BF16)],
        compiler_params=_params(1),
        name="proj_norm",
    )(p, w, g.reshape(1, n).astype(F32))


def _layer(x, p, w_in, conv_w, a_log, dt_bias, dn_norm, w_attn_up, w_dn_up, w_out, w_mlp_up,
           w_mlp_down, w_ple_gate, w_ple_proj, norm_mix, norm_mlp, norm_ple, ple_post_norm):
    s_len, d_model = x.shape
    attn_w = ATTN_HEADS * HEAD_DIM
    dn_w = DN_HEADS * HEAD_DIM
    n_groups = len(ATTN_DILATIONS)
    c_qkv_b = 3 * n_groups * attn_w
    c_z = c_qkv_b + 3 * dn_w
    c_ba = c_z + dn_w
    c_gate = c_ba + 2 * DN_HEADS
    mm = matmul_ws

    h = rmsnorm(x, norm_mix, BF16)
    qkv_a = mm(h, w_in, col0=0, n=c_qkv_b, out_dtype=BF16, name="proj_attn")
    qkv_b = mm(h, w_in, col0=c_qkv_b, n=3 * dn_w, out_dtype=F32, name="proj_dn")
    z_b = mm(h, w_in, col0=c_z, n=dn_w, out_dtype=F32, name="proj_z")
    ba = mm(h, w_in, col0=c_ba, n=512, out_dtype=F32, name="proj_ba")
    gates = mm(h, w_in[:, c_gate:], out_dtype=BF16, name="proj_gates")

    outs, lses = [], []
    for gi, dil in enumerate(ATTN_DILATIONS):
        if dil == 1:
            arr = qkv_a.reshape(1, s_len, c_qkv_b)
            o_g, lse_g = banded_attention(
                arr, arr, arr, col_blocks=(gi, n_groups + gi, 2 * n_groups + gi))
        else:
            def sub(t):
                cols = qkv_a[:, (t * n_groups + gi) * attn_w:(t * n_groups + gi + 1) * attn_w]
                return cols.reshape(s_len // dil, dil, attn_w).transpose(1, 0, 2)
            o_g, lse_g = banded_attention(sub(0), sub(1), sub(2))
        outs.append(o_g.transpose(1, 0, 2).reshape(s_len, attn_w))
        lses.append(lse_g.transpose(1, 0, 2).reshape(s_len, HEAD_DIM))
    o_a = attn_combine(outs, lses)

    o_b = gated_deltanet(qkv_b, conv_w, ba, a_log, dt_bias, z_b, dn_norm)

    t_a = mm(o_a, w_attn_up, out_dtype=F32, epilogue="gate", extra=((gates, 0),), name="attn_up")
    merged = mm(o_b, w_dn_up, out_dtype=BF16, epilogue="gate_add",
                extra=((gates, d_model), (t_a, 0)), name="dn_up")
    x = mm(merged, w_out, out_dtype=F32, epilogue="residual", extra=((x, 0),), name="out_proj")

    h = rmsnorm(x, norm_mlp, BF16)
    u = mm(h, w_mlp_up, out_dtype=BF16, epilogue="relu2", name="mlp_up")
    x = mm(u, w_mlp_down, out_dtype=F32, epilogue="residual", extra=((x, 0),), tk=2048,
           name="mlp_down")

    h = rmsnorm(x, norm_ple, BF16)
    pp = proj_norm(p, w_ple_proj, ple_post_norm, F32)
    x = mm(h, w_ple_gate, out_dtype=F32, epilogue="ple", extra=((x, 0), (pp, 0)), name="ple_gate")
    return x


def kernel(x, p, w_in, conv_w, dn_a_log, dn_dt_bias, dn_norm, w_attn_up, w_dn_up, w_out,
           w_mlp_up, w_mlp_down, w_ple_gate, w_ple_proj, norm_mix, norm_mlp, norm_ple,
           ple_post_norm, final_norm):
    b, s_len, d_model = x.shape
    assert b == 1
    depth = w_in.shape[0]
    xs = x.reshape(s_len, d_model)
    for i in range(depth):
        xs = _layer(xs, p[i, 0], w_in[i], conv_w[i], dn_a_log[i], dn_dt_bias[i], dn_norm[i],
                    w_attn_up[i], w_dn_up[i], w_out[i], w_mlp_up[i], w_mlp_down[i],
                    w_ple_gate[i], w_ple_proj[i], norm_mix[i], norm_mlp[i], norm_ple[i],
                    ple_post_norm[i])
    out = rmsnorm(xs, final_norm, x.dtype)
    return out.reshape(b, s_len, d_model)
```

```python
import functools
import math

import jax
import jax.numpy as jnp
from jax import lax
from jax.experimental import pallas as pl
from jax.experimental.pallas import tpu as pltpu

F32 = jnp.float32
BF16 = jnp.bfloat16

EPS = 1e-6
LANES = 128
HEAD_DIM = 128
ATTN_SPAN = 128
ATTN_DILATIONS = (1, 4, 16)
ATTN_HEADS = 8
DN_HEADS = 16
DN_CHUNK = 64
CONV_K = 4

VMEM_LIMIT_BYTES = 56 * 1024 * 1024


def _params(n_axes):
    return pltpu.CompilerParams(
        dimension_semantics=("arbitrary",) * n_axes,
        vmem_limit_bytes=VMEM_LIMIT_BYTES,
    )


def _rmsnorm_kernel(x_ref, g_ref, o_ref):
    x = x_ref[...].astype(F32)
    ms = jnp.mean(x * x, axis=-1, keepdims=True)
    o_ref[...] = (x * lax.rsqrt(ms + EPS) * g_ref[...]).astype(o_ref.dtype)


def rmsnorm(x, g, out_dtype, *, tm=256):
    m, d = x.shape
    return pl.pallas_call(
        _rmsnorm_kernel,
        grid=(m // tm,),
        in_specs=[pl.BlockSpec((tm, d), lambda i: (i, 0)),
                  pl.BlockSpec((1, d), lambda i: (0, 0))],
        out_specs=pl.BlockSpec((tm, d), lambda i: (i, 0)),
        out_shape=jax.ShapeDtypeStruct((m, d), out_dtype),
        compiler_params=_params(1),
        name="rmsnorm",
    )(x, g.reshape(1, d).astype(F32))


def _sigmoid(x):
    return 1.0 / (1.0 + jnp.exp(-x))


def _mm_kernel(*refs, nk, tm, epilogue, n_extra, lane_shift):
    a_ref, b_ref = refs[0], refs[1]
    n_in = 2
    if lane_shift:
        bnext_ref = refs[2]
        n_in = 3
    extra = refs[n_in:n_in + n_extra]
    o_ref = refs[n_in + n_extra]
    bcast_ref = refs[n_in + 1 + n_extra]
    acc_ref = refs[n_in + 2 + n_extra] if nk > 1 else None
    k = pl.program_id(1)
    i = pl.program_id(2)

    @pl.when(i == 0)
    def _():
        if lane_shift:
            tn = b_ref.shape[1]
            bcast_ref[:, :tn - lane_shift] = b_ref[:, lane_shift:].astype(BF16)
            bcast_ref[:, tn - lane_shift:] = bnext_ref[:, :lane_shift].astype(BF16)
        else:
            bcast_ref[...] = b_ref[...].astype(BF16)

    part = jnp.dot(a_ref[...], bcast_ref[...], preferred_element_type=F32)

    def finish(acc):
        if epilogue == "plain":
            val = acc
        elif epilogue == "relu2":
            r = jnp.maximum(acc, 0.0)
            val = r * r
        elif epilogue == "residual":
            val = extra[0][...].astype(F32) + acc
        elif epilogue == "gate":
            val = _sigmoid(extra[0][...].astype(F32)) * acc
        elif epilogue == "gate_add":
            val = extra[1][...].astype(F32) + _sigmoid(extra[0][...].astype(F32)) * acc
        elif epilogue == "ple":
            val = extra[0][...].astype(F32) + _sigmoid(acc) * extra[1][...].astype(F32)
        else:
            raise ValueError(epilogue)
        o_ref[...] = val.astype(o_ref.dtype)

    if nk == 1:
        finish(part)
    else:
        rows = pl.ds(pl.multiple_of(i * tm, tm), tm)

        @pl.when(k == 0)
        def _():
            acc_ref[rows, :] = part

        @pl.when(jnp.logical_and(k > 0, k < nk - 1))
        def _():
            acc_ref[rows, :] += part

        @pl.when(k == nk - 1)
        def _():
            finish(acc_ref[rows, :] + part)


def matmul_ws(a, b, *, col0=0, n=None, out_dtype=BF16, epilogue="plain", extra=(),
              tm=1024, tn=512, tk=None, lane_shift=0, name="matmul_ws"):
    m, kdim = a.shape
    if n is None:
        n = b.shape[1] - col0
    if tk is None:
        tk = kdim
    assert m % tm == 0 and n % tn == 0 and kdim % tk == 0 and col0 % tn == 0
    assert 0 <= lane_shift < LANES and col0 + lane_shift + n <= b.shape[1]
    nj, nk, ni = n // tn, kdim // tk, m // tm
    j0 = col0 // tn

    def out_map(j, k, i):
        return (jnp.where(k == nk - 1, i, 0), j)

    in_specs = [pl.BlockSpec((tm, tk), lambda j, k, i: (i, k)),
                pl.BlockSpec((tk, tn), lambda j, k, i: (k, j + j0))]
    operands = [a, b]
    if lane_shift:
        in_specs.append(pl.BlockSpec((tk, LANES),
                                     lambda j, k, i: (k, (j + j0 + 1) * (tn // LANES))))
        operands.append(b)
    for _, ecol0 in extra:
        assert ecol0 % tn == 0
        in_specs.append(pl.BlockSpec(
            (tm, tn), functools.partial(lambda j, k, i, e0: (out_map(j, k, i)[0], j + e0),
                                        e0=ecol0 // tn)))
    scratch = [pltpu.VMEM((tk, tn), BF16)]
    if nk > 1:
        scratch.append(pltpu.VMEM((m, tn), F32))
    return pl.pallas_call(
        functools.partial(_mm_kernel, nk=nk, tm=tm, epilogue=epilogue, n_extra=len(extra),
                          lane_shift=lane_shift),
        grid=(nj, nk, ni),
        in_specs=in_specs,
        out_specs=pl.BlockSpec((tm, tn), out_map),
        out_shape=jax.ShapeDtypeStruct((m, n), out_dtype),
        scratch_shapes=scratch,
        compiler_params=_params(3),
        name=name,
    )(*operands, *[arr for arr, _ in extra])


def _attn_kernel(q_ref, kp_ref, kc_ref, vp_ref, vc_ref, o_ref, lse_ref, *, heads, scale):
    n = pl.program_id(1)
    span = q_ref.shape[0]
    qi = lax.broadcasted_iota(jnp.int32, (span, 2 * span), 0)
    kj = lax.broadcasted_iota(jnp.int32, (span, 2 * span), 1)
    dist = qi + span - kj
    valid = (dist >= 0) & (dist <= span) & ((kj >= span) | (n > 0))
    lane = lax.broadcasted_iota(jnp.int32, (span, HEAD_DIM), 1)
    lse_tile = jnp.zeros((span, HEAD_DIM), F32)
    for h in range(heads):
        cols = slice(h * HEAD_DIM, (h + 1) * HEAD_DIM)
        q = q_ref[:, cols]
        k = jnp.concatenate([kp_ref[:, cols], kc_ref[:, cols]], axis=0)
        v = jnp.concatenate([vp_ref[:, cols], vc_ref[:, cols]], axis=0)
        s = lax.dot_general(q, k, (((1,), (1,)), ((), ())), preferred_element_type=F32) * scale
        s = jnp.where(valid, s, -jnp.inf)
        m = jnp.max(s, axis=-1, keepdims=True)
        e = jnp.exp(s - m)
        l = jnp.sum(e, axis=-1, keepdims=True)
        o = jnp.dot(e.astype(BF16), v, preferred_element_type=F32) / l
        o_ref[:, cols] = o.astype(o_ref.dtype)
        lse_tile = jnp.where(lane == h, m + jnp.log(l), lse_tile)
    lse_ref[...] = lse_tile


def banded_attention(q, k, v, *, heads=ATTN_HEADS, col_blocks=(0, 0, 0)):
    d, length, _ = q.shape
    width = heads * HEAD_DIM
    span = ATTN_SPAN
    nb = length // span

    def cur_spec(cb):
        return pl.BlockSpec((None, span, width), lambda r, n: (r, n, cb))

    def prev_spec(cb):
        return pl.BlockSpec((None, span, width), lambda r, n: (r, jnp.maximum(n - 1, 0), cb))

    qc, kc, vc = col_blocks
    return pl.pallas_call(
        functools.partial(_attn_kernel, heads=heads, scale=HEAD_DIM ** -0.5),
        grid=(d, nb),
        in_specs=[cur_spec(qc), prev_spec(kc), cur_spec(kc), prev_spec(vc), cur_spec(vc)],
        out_specs=[pl.BlockSpec((None, span, width), lambda r, n: (r, n, 0)),
                   pl.BlockSpec((None, span, HEAD_DIM), lambda r, n: (r, n, 0))],
        out_shape=[jax.ShapeDtypeStruct((d, length, width), F32),
                   jax.ShapeDtypeStruct((d, length, HEAD_DIM), F32)],
        compiler_params=_params(2),
        name="banded_attention",
    )(q, k, k, v, v)


def _attn_combine_kernel(o0_ref, o1_ref, o2_ref, l0_ref, l1_ref, l2_ref, out_ref, *, heads):
    l0, l1, l2 = l0_ref[...], l1_ref[...], l2_ref[...]
    m = jnp.maximum(jnp.maximum(l0, l1), l2)
    e0, e1, e2 = jnp.exp(l0 - m), jnp.exp(l1 - m), jnp.exp(l2 - m)
    den = e0 + e1 + e2
    w0, w1, w2 = e0 / den, e1 / den, e2 / den
    for h in range(heads):
        cols = slice(h * HEAD_DIM, (h + 1) * HEAD_DIM)
        acc = (w0[:, h:h + 1] * o0_ref[:, cols] + w1[:, h:h + 1] * o1_ref[:, cols]
               + w2[:, h:h + 1] * o2_ref[:, cols])
        out_ref[:, cols] = acc.astype(out_ref.dtype)


def attn_combine(outs, lses, *, tm=256):
    m, width = outs[0].shape
    heads = width // HEAD_DIM
    ospec = pl.BlockSpec((tm, width), lambda i: (i, 0))
    lspec = pl.BlockSpec((tm, HEAD_DIM), lambda i: (i, 0))
    return pl.pallas_call(
        functools.partial(_attn_combine_kernel, heads=heads),
        grid=(m // tm,),
        in_specs=[ospec] * 3 + [lspec] * 3,
        out_specs=ospec,
        out_shape=jax.ShapeDtypeStruct((m, width), BF16),
        compiler_params=_params(1),
        name="attn_combine",
    )(*outs, *lses)


def _softplus(x):
    return jnp.maximum(x, 0.0) + jnp.log(1.0 + jnp.exp(-jnp.abs(x)))


def _bdot(a, b, dims):
    return lax.dot_general(a, b, (dims, ((0,), (0,))), preferred_element_type=F32)


def _bmm(a, b):
    return _bdot(a, b, ((2,), (1,)))


def _split_bf16(x):
    hi = x.astype(BF16)
    lo = (x - hi.astype(F32)).astype(BF16)
    return hi, lo


def _bmm_3pass(a, b):
    a_hi, a_lo = _split_bf16(a)
    b_hi, b_lo = _split_bf16(b)
    return _bmm(a_hi, b_hi) + _bmm(a_hi, b_lo) + _bmm(a_lo, b_hi)


def _unit_lower_inverse_batched(a):
    c = a.shape[-1]
    eye = (lax.broadcasted_iota(jnp.int32, (c, c), 0)
           == lax.broadcasted_iota(jnp.int32, (c, c), 1)).astype(F32)
    p = eye - a
    x = a
    for _ in range(int(math.log2(c)) - 1):
        x = _bmm_3pass(x, x)
        p = p + _bmm_3pass(p, x)
    return p


def _dn_prep_kernel(qp_ref, qh_ref, kp_ref, kh_ref, vp_ref, vh_ref, cwq_ref, cwk_ref, cwv_ref,
                    ba_ref, bat_ref, prow_ref, pcol_ref,
                    u_ref, w_ref, qg_ref, kd_ref, intra_ref, eg_ref, *, heads, chunks):
    tb = pl.program_id(0)
    c_len = DN_CHUNK
    rows_total = chunks * c_len

    def conv_silu(x_ref, halo_ref, cw_ref):
        halo = jnp.where(tb == 0, 0.0, halo_ref[...])
        xx = jnp.concatenate([halo, x_ref[...]], axis=0)
        cw = cw_ref[...]
        base = halo.shape[0] - (CONV_K - 1)
        y = xx[base:base + rows_total] * cw[0:1]
        for j in range(1, CONV_K):
            y = y + xx[base + j:base + j + rows_total] * cw[j:j + 1]
        return y * _sigmoid(y)

    qc = conv_silu(qp_ref, qh_ref, cwq_ref)
    kc = conv_silu(kp_ref, kh_ref, cwk_ref)
    vc = conv_silu(vp_ref, vh_ref, cwv_ref)

    ba = ba_ref[...]
    beta_col = _sigmoid(ba)
    g_col = -jnp.exp(prow_ref[0:1, :]) * _softplus(ba + prow_ref[1:2, :])
    row_in_chunk = lax.broadcasted_iota(jnp.int32, g_col.shape, 0) % c_len
    shift = 1
    while shift < c_len:
        g_col = g_col + jnp.where(row_in_chunk >= shift, pltpu.roll(g_col, shift, 0), 0.0)
        shift *= 2
    bat = bat_ref[...]
    beta_row = _sigmoid(bat)
    g_row = -jnp.exp(pcol_ref[:, 0:1])[None] * _softplus(bat + pcol_ref[:, 1:2][None])
    lane_in_chunk = lax.broadcasted_iota(jnp.int32, g_row.shape, 2)
    shift = 1
    while shift < c_len:
        g_row = g_row + jnp.where(lane_in_chunk >= shift, pltpu.roll(g_row, shift, 2), 0.0)
        shift *= 2
    eg_ref[...] = jnp.exp(g_row)

    ri = lax.broadcasted_iota(jnp.int32, (c_len, c_len), 0)
    ci = lax.broadcasted_iota(jnp.int32, (c_len, c_len), 1)
    lower = ri >= ci
    strict = ri > ci

    for c in range(chunks):
        rs = slice(c * c_len, (c + 1) * c_len)

        def per_head(x):
            return jnp.stack([x[rs, h * HEAD_DIM:(h + 1) * HEAD_DIM] for h in range(heads)])

        q, k, v = per_head(qc), per_head(kc), per_head(vc)
        q = q * (lax.rsqrt(jnp.sum(q * q, axis=-1, keepdims=True) + EPS) * HEAD_DIM ** -0.5)
        k = k * lax.rsqrt(jnp.sum(k * k, axis=-1, keepdims=True) + EPS)
        gcol = jnp.stack([g_col[rs, heads + h:heads + h + 1] for h in range(heads)])
        bcol = jnp.stack([beta_col[rs, h:h + 1] for h in range(heads)])
        grow = jnp.stack([g_row[c, heads + h:heads + h + 1, :] for h in range(heads)])
        brow = jnp.stack([beta_row[c, h:h + 1, :] for h in range(heads)])
        decay = jnp.exp(jnp.where(lower, gcol - grow, -jnp.inf))
        q16, k16, v16 = q.astype(BF16), k.astype(BF16), v.astype(BF16)
        qk = _bdot(jnp.concatenate([q16, k16], axis=1), k16, ((2,), (2,)))
        intra = qk[:, :c_len] * decay
        a = jnp.where(strict, qk[:, c_len:] * decay * bcol, 0.0)
        t_inv = _unit_lower_inverse_batched(a)
        u = _bmm((t_inv * brow).astype(BF16), v16)
        w = _bmm((t_inv * (brow * jnp.exp(grow))).astype(BF16), k16)
        g_last = grow[:, :, c_len - 1:c_len]
        u_ref[:, rs, :] = u
        w_ref[:, rs, :] = w.astype(w_ref.dtype)
        qg_ref[:, rs, :] = (q * jnp.exp(gcol)).astype(qg_ref.dtype)
        kd_ref[:, rs, :] = (k * jnp.exp(g_last - gcol)).astype(kd_ref.dtype)
        intra_ref[:, rs, :] = intra.astype(intra_ref.dtype)


def _dn_scan_kernel(egl_ref, u_ref, w_ref, qg_ref, kd_ref, intra_ref, z_ref, gn_ref, o_ref,
                    state_ref, *, heads, chunks):
    tb = pl.program_id(0)
    c_len = DN_CHUNK

    @pl.when(tb == 0)
    def _():
        state_ref[...] = jnp.zeros_like(state_ref)

    gn = gn_ref[...]
    for c in range(chunks):
        rs = slice(c * c_len, (c + 1) * c_len)
        s = state_ref[...]
        lhs = jnp.concatenate([w_ref[:, rs, :], qg_ref[:, rs, :]], axis=1)
        r = _bmm(lhs, s.astype(BF16))
        v_new = u_ref[:, rs, :] - r[:, :c_len]
        vn16 = v_new.astype(BF16)
        o = r[:, c_len:] + _bmm(intra_ref[:, rs, :], vn16)
        upd = _bdot(kd_ref[:, rs, :], vn16, ((1,), (1,)))
        base = (tb * chunks + c) * heads
        for h in range(heads):
            state_ref[h] = s[h] * egl_ref[base + h] + upd[h]
        o = o * lax.rsqrt(jnp.mean(o * o, axis=-1, keepdims=True) + EPS) * gn
        for h in range(heads):
            cs = slice(h * HEAD_DIM, (h + 1) * HEAD_DIM)
            zz = z_ref[rs, cs].astype(F32)
            o_ref[rs, cs] = (o[h] * (zz * _sigmoid(zz))).astype(o_ref.dtype)


def gated_deltanet2(qkv_pre, conv_w, ba, a_log, dt_bias, z, dn_norm, *, prep_chunks=2,
                    scan_chunks=4):
    s_len, width = z.shape
    heads = width // HEAD_DIM
    c_len = DN_CHUNK
    n_chunks = s_len // c_len
    halo = 8
    bat = ba[:, :2 * heads].reshape(n_chunks, c_len, 2 * heads).transpose(0, 2, 1)
    prow = jnp.zeros((2, HEAD_DIM), F32)
    prow = prow.at[0, heads:2 * heads].set(a_log.astype(F32))
    prow = prow.at[1, heads:2 * heads].set(dt_bias.astype(F32))
    pcol = jnp.zeros((2 * heads, 2), F32)
    pcol = pcol.at[heads:, 0].set(a_log.astype(F32)).at[heads:, 1].set(dt_bias.astype(F32))

    rows = prep_chunks * c_len

    def main_spec(i):
        return pl.BlockSpec((rows, width), lambda t: (t, i))

    def halo_spec(i):
        return pl.BlockSpec((halo, width), lambda t: (jnp.maximum(t * (rows // halo) - 1, 0), i))

    cw_specs = [pl.BlockSpec((CONV_K, width), functools.partial(lambda t, i: (0, i), i=i))
                for i in range(3)]

    def prep_out(last):
        return pl.BlockSpec((heads, rows, last), lambda t: (0, t, 0))

    u, w, qg, kd, intra, eg = pl.pallas_call(
        functools.partial(_dn_prep_kernel, heads=heads, chunks=prep_chunks),
        grid=(s_len // rows,),
        in_specs=[main_spec(0), halo_spec(0), main_spec(1), halo_spec(1), main_spec(2),
                  halo_spec(2), *cw_specs,
                  pl.BlockSpec((rows, HEAD_DIM), lambda t: (t, 0)),
                  pl.BlockSpec((prep_chunks, 2 * heads, c_len), lambda t: (t, 0, 0)),
                  pl.BlockSpec((2, HEAD_DIM), lambda t: (0, 0)),
                  pl.BlockSpec((2 * heads, 2), lambda t: (0, 0))],
        out_specs=[prep_out(HEAD_DIM), prep_out(HEAD_DIM), prep_out(HEAD_DIM),
                   prep_out(HEAD_DIM), prep_out(c_len),
                   pl.BlockSpec((prep_chunks, 2 * heads, c_len), lambda t: (t, 0, 0))],
        out_shape=[jax.ShapeDtypeStruct((heads, s_len, HEAD_DIM), F32),
                   jax.ShapeDtypeStruct((heads, s_len, HEAD_DIM), BF16),
                   jax.ShapeDtypeStruct((heads, s_len, HEAD_DIM), BF16),
                   jax.ShapeDtypeStruct((heads, s_len, HEAD_DIM), BF16),
                   jax.ShapeDtypeStruct((heads, s_len, c_len), BF16),
                   jax.ShapeDtypeStruct((n_chunks, 2 * heads, c_len), F32)],
        compiler_params=_params(1),
        name="deltanet_prep",
    )(qkv_pre, qkv_pre, qkv_pre, qkv_pre, qkv_pre, qkv_pre, conv_w, conv_w, conv_w, ba, bat,
      prow, pcol)

    egl = eg[:, heads:, c_len - 1].reshape(n_chunks * heads)

    scan_rows = scan_chunks * c_len

    def head_major(last):
        return pl.BlockSpec((heads, scan_rows, last), lambda t: (0, t, 0))

    nat = pl.BlockSpec((scan_rows, width), lambda t: (t, 0))
    return pl.pallas_call(
        functools.partial(_dn_scan_kernel, heads=heads, chunks=scan_chunks),
        grid=(s_len // scan_rows,),
        in_specs=[pl.BlockSpec(memory_space=pltpu.SMEM),
                  head_major(HEAD_DIM), head_major(HEAD_DIM), head_major(HEAD_DIM),
                  head_major(HEAD_DIM), head_major(c_len), nat,
                  pl.BlockSpec((1, HEAD_DIM), lambda t: (0, 0))],
        out_specs=nat,
        out_shape=jax.ShapeDtypeStruct((s_len, width), BF16),
        scratch_shapes=[pltpu.VMEM((heads, HEAD_DIM, HEAD_DIM), F32)],
        compiler_params=_params(1),
        name="deltanet_scan",
    )(egl, u, w, qg, kd, intra, z, dn_norm.reshape(1, HEAD_DIM).astype(F32))


def _proj_norm_kernel(p_ref, w_ref, g_ref, o_ref, wcast_ref):
    @pl.when(pl.program_id(0) == 0)
    def _():
        wcast_ref[...] = w_ref[...].astype(BF16)

    y = jnp.dot(p_ref[...].astype(BF16), wcast_ref[...], preferred_element_type=F32)
    ms = jnp.mean(y * y, axis=-1, keepdims=True)
    o_ref[...] = (y * lax.rsqrt(ms + EPS) * g_ref[...]).astype(o_ref.dtype)


def proj_norm(p, w, g, out_dtype, *, tm=512):
    m, kdim = p.shape
    n = w.shape[1]
    return pl.pallas_call(
        _proj_norm_kernel,
        grid=(m // tm,),
        in_specs=[pl.BlockSpec((tm, kdim), lambda i: (i, 0)),
                  pl.BlockSpec((kdim, n), lambda i: (0, 0)),
                  pl.BlockSpec((1, n), lambda i: (0, 0))],
        out_specs=pl.BlockSpec((tm, n), lambda i: (i, 0)),
        out_shape=jax.ShapeDtypeStruct((m, n), out_dtype),
        scratch_shapes=[pltpu.VMEM((kdim, n), BF16)],
        compiler_params=_params(1),
        name="proj_norm",
    )(p, w, g.reshape(1, n).astype(F32))


def _layer(x, p, w_in, conv_w, a_log, dt_bias, dn_norm, w_attn_up, w_dn_up, w_out, w_mlp_up,
           w_mlp_down, w_ple_gate, w_ple_proj, norm_mix, norm_mlp, norm_ple, ple_post_norm):
    s_len, d_model = x.shape
    attn_w = ATTN_HEADS * HEAD_DIM
    dn_w = DN_HEADS * HEAD_DIM
    n_groups = len(ATTN_DILATIONS)
    c_qkv_b = 3 * n_groups * attn_w
    c_z = c_qkv_b + 3 * dn_w
    c_ba = c_z + dn_w
    c_gate = c_ba + 2 * DN_HEADS
    mm = matmul_ws

    h = rmsnorm(x, norm_mix, BF16)
    qkv_a = mm(h, w_in, col0=0, n=c_qkv_b, out_dtype=BF16, name="proj_attn")
    qkv_b = mm(h, w_in, col0=c_qkv_b, n=3 * dn_w, out_dtype=F32, name="proj_dn")
    z_b = mm(h, w_in, col0=c_z, n=dn_w, out_dtype=F32, name="proj_z")
    ba = mm(h, w_in, col0=c_ba, n=512, out_dtype=F32, name="proj_ba")
    gates = mm(h, w_in, col0=c_ba, lane_shift=c_gate - c_ba, n=2 * d_model, out_dtype=BF16,
               name="proj_gates")

    outs, lses = [], []
    for gi, dil in enumerate(ATTN_DILATIONS):
        if dil == 1:
            arr = qkv_a.reshape(1, s_len, c_qkv_b)
            o_g, lse_g = banded_attention(
                arr, arr, arr, col_blocks=(gi, n_groups + gi, 2 * n_groups + gi))
        else:
            def sub(t):
                cols = qkv_a[:, (t * n_groups + gi) * attn_w:(t * n_groups + gi + 1) * attn_w]
                return cols.reshape(s_len // dil, dil, attn_w).transpose(1, 0, 2)
            o_g, lse_g = banded_attention(sub(0), sub(1), sub(2))
        outs.append(o_g.transpose(1, 0, 2).reshape(s_len, attn_w))
        lses.append(lse_g.transpose(1, 0, 2).reshape(s_len, HEAD_DIM))
    o_a = attn_combine(outs, lses)

    o_b = gated_deltanet2(qkv_b, conv_w, ba, a_log, dt_bias, z_b, dn_norm)

    t_a = mm(o_a, w_attn_up, out_dtype=F32, epilogue="gate", extra=((gates, 0),), name="attn_up")
    merged = mm(o_b, w_dn_up, out_dtype=BF16, epilogue="gate_add",
                extra=((gates, d_model), (t_a, 0)), name="dn_up")
    x = mm(merged, w_out, out_dtype=F32, epilogue="residual", extra=((x, 0),), name="out_proj")

    h = rmsnorm(x, norm_mlp, BF16)
    u = mm(h, w_mlp_up, out_dtype=BF16, epilogue="relu2", name="mlp_up")
    x = mm(u, w_mlp_down, out_dtype=F32, epilogue="residual", extra=((x, 0),), tk=2048,
           name="mlp_down")

    h = rmsnorm(x, norm_ple, BF16)
    pp = proj_norm(p, w_ple_proj, ple_post_norm, F32)
    x = mm(h, w_ple_gate, out_dtype=F32, epilogue="ple", extra=((x, 0), (pp, 0)), name="ple_gate")
    return x


def kernel(x, p, w_in, conv_w, dn_a_log, dn_dt_bias, dn_norm, w_attn_up, w_dn_up, w_out,
           w_mlp_up, w_mlp_down, w_ple_gate, w_ple_proj, norm_mix, norm_mlp, norm_ple,
           ple_post_norm, final_norm):
    b, s_len, d_model = x.shape
    assert b == 1
    depth = w_in.shape[0]
    xs = x.reshape(s_len, d_model)
    for i in range(depth):
        xs = _layer(xs, p[i, 0], w_in[i], conv_w[i], dn_a_log[i], dn_dt_bias[i], dn_norm[i],
                    w_attn_up[i], w_dn_up[i], w_out[i], w_mlp_up[i], w_mlp_down[i],
                    w_ple_gate[i], w_ple_proj[i], norm_mix[i], norm_mlp[i], norm_ple[i],
                    ple_post_norm[i])
    out = rmsnorm(xs, final_norm, x.dtype)
    return out.reshape(b, s_len, d_model)
```

```python
import functools
import math

import jax
import jax.numpy as jnp
from jax import lax
from jax.experimental import pallas as pl
from jax.experimental.pallas import tpu as pltpu

F32 = jnp.float32
BF16 = jnp.bfloat16

EPS = 1e-6
LANES = 128
SUBLANES = 8
HEAD_DIM = 128
ATTN_SPAN = 128
ATTN_DILATIONS = (1, 4, 16)
ATTN_HEADS = 8
DN_HEADS = 16
DN_CHUNK = 64
CONV_K = 4

VMEM_LIMIT_BYTES = 56 * 1024 * 1024


def _params(n_axes):
    return pltpu.CompilerParams(
        dimension_semantics=("arbitrary",) * n_axes,
        vmem_limit_bytes=VMEM_LIMIT_BYTES,
    )


def _rmsnorm_kernel(x_ref, g_ref, o_ref):
    x = x_ref[...].astype(F32)
    ms = jnp.mean(x * x, axis=-1, keepdims=True)
    o_ref[...] = (x * lax.rsqrt(ms + EPS) * g_ref[...]).astype(o_ref.dtype)


def rmsnorm(x, g, out_dtype, *, tm=256):
    m, d = x.shape
    return pl.pallas_call(
        _rmsnorm_kernel,
        grid=(m // tm,),
        in_specs=[pl.BlockSpec((tm, d), lambda i: (i, 0)),
                  pl.BlockSpec((1, d), lambda i: (0, 0))],
        out_specs=pl.BlockSpec((tm, d), lambda i: (i, 0)),
        out_shape=jax.ShapeDtypeStruct((m, d), out_dtype),
        compiler_params=_params(1),
        name="rmsnorm",
    )(x, g.reshape(1, d).astype(F32))


def _sigmoid(x):
    return 1.0 / (1.0 + jnp.exp(-x))


def _mm_kernel(*refs, nk, tm, epilogue, n_extra, b_is_nk):
    a_ref, b_ref = refs[0], refs[1]
    extra = refs[2:2 + n_extra]
    o_ref = refs[2 + n_extra]
    bcast_ref = refs[3 + n_extra]
    tail_refs = refs[4 + n_extra:]
    acc_ref = tail_refs[0] if nk > 1 else None
    carry_ref = tail_refs[-1] if epilogue.startswith("conv_silu") else None
    k = pl.program_id(1)
    i = pl.program_id(2)

    @pl.when(i == 0)
    def _():
        bcast_ref[...] = b_ref[...].astype(BF16)

    if b_is_nk:
        part = lax.dot_general(a_ref[...], bcast_ref[...], (((1,), (1,)), ((), ())),
                               preferred_element_type=F32)
    else:
        part = jnp.dot(a_ref[...], bcast_ref[...], preferred_element_type=F32)

    def finish(acc):
        if epilogue == "plain":
            val = acc
        elif epilogue == "relu2":
            r = jnp.maximum(acc, 0.0)
            val = r * r
        elif epilogue == "residual":
            val = extra[0][...].astype(F32) + acc
        elif epilogue == "gate":
            val = _sigmoid(extra[0][...].astype(F32)) * acc
        elif epilogue == "gate_add":
            val = extra[1][...].astype(F32) + _sigmoid(extra[0][...].astype(F32)) * acc
        elif epilogue == "ple":
            val = extra[0][...].astype(F32) + _sigmoid(acc) * extra[1][...].astype(F32)
        elif epilogue.startswith("conv_silu"):
            prev = jnp.where(i == 0, 0.0, carry_ref[...])
            xx = jnp.concatenate([prev, acc], axis=0)
            cw = extra[0][...]
            base = prev.shape[0] - (CONV_K - 1)
            y = xx[base:base + tm] * cw[0:1]
            for tap in range(1, CONV_K):
                y = y + xx[base + tap:base + tap + tm] * cw[tap:tap + 1]
            carry_ref[...] = acc[tm - prev.shape[0]:, :]
            y = y * _sigmoid(y)
            if epilogue == "conv_silu_l2norm":
                heads_in_tile = y.shape[1] // HEAD_DIM
                segs = []
                for hh in range(heads_in_tile):
                    seg = y[:, hh * HEAD_DIM:(hh + 1) * HEAD_DIM]
                    segs.append(seg * lax.rsqrt(jnp.sum(seg * seg, axis=-1, keepdims=True) + EPS))
                y = jnp.concatenate(segs, axis=1)
            val = y
        else:
            raise ValueError(epilogue)
        o_ref[...] = val.astype(o_ref.dtype)

    if nk == 1:
        finish(part)
    else:
        rows = pl.ds(pl.multiple_of(i * tm, tm), tm)

        @pl.when(k == 0)
        def _():
            acc_ref[rows, :] = part

        @pl.when(jnp.logical_and(k > 0, k < nk - 1))
        def _():
            acc_ref[rows, :] += part

        @pl.when(k == nk - 1)
        def _():
            finish(acc_ref[rows, :] + part)


def matmul_ws(a, b, *, col0=0, n=None, out_dtype=BF16, epilogue="plain", extra=(),
              tm=1024, tn=512, tk=None, b_is_nk=False, name="matmul_ws"):
    m, kdim = a.shape
    n_total = b.shape[0] if b_is_nk else b.shape[1]
    if n is None:
        n = n_total - col0
    if tk is None:
        tk = kdim
    assert m % tm == 0 and n % tn == 0 and kdim % tk == 0 and col0 + n <= n_total
    nj, nk, ni = n // tn, kdim // tk, m // tm

    def out_map(j, k, i):
        return (jnp.where(k == nk - 1, i, 0), j)

    if b_is_nk:
        assert col0 % SUBLANES == 0
        b_spec = pl.BlockSpec(
            (pl.Element(tn), pl.Element(tk)),
            lambda j, k, i: (pl.multiple_of(col0 + j * tn, SUBLANES), pl.multiple_of(k * tk, LANES)))
        bcast_shape = (tn, tk)
    else:
        assert col0 % tn == 0
        b_spec = pl.BlockSpec((tk, tn), lambda j, k, i: (k, j + col0 // tn))
        bcast_shape = (tk, tn)
    in_specs = [pl.BlockSpec((tm, tk), lambda j, k, i: (i, k)), b_spec]
    conv = epilogue.startswith("conv_silu")
    for idx, (_, ecol0) in enumerate(extra):
        assert ecol0 % tn == 0
        if conv and idx == 0:
            in_specs.append(pl.BlockSpec(
                (CONV_K, tn), functools.partial(lambda j, k, i, e0: (0, j + e0), e0=ecol0 // tn)))
        else:
            in_specs.append(pl.BlockSpec(
                (tm, tn), functools.partial(lambda j, k, i, e0: (out_map(j, k, i)[0], j + e0),
                                            e0=ecol0 // tn)))
    scratch = [pltpu.VMEM(bcast_shape, BF16)]
    if nk > 1:
        scratch.append(pltpu.VMEM((m, tn), F32))
    if conv:
        assert nk == 1
        scratch.append(pltpu.VMEM((SUBLANES, tn), F32))
    return pl.pallas_call(
        functools.partial(_mm_kernel, nk=nk, tm=tm, epilogue=epilogue, n_extra=len(extra),
                          b_is_nk=b_is_nk),
        grid=(nj, nk, ni),
        in_specs=in_specs,
        out_specs=pl.BlockSpec((tm, tn), out_map),
        out_shape=jax.ShapeDtypeStruct((m, n), out_dtype),
        scratch_shapes=scratch,
        compiler_params=_params(3),
        name=name,
    )(a, b, *[arr for arr, _ in extra])


def _attn_kernel(q_ref, kp_ref, kc_ref, vp_ref, vc_ref, o_ref, lse_ref, *, heads, scale):
    n = pl.program_id(1)
    span = q_ref.shape[0]
    qi = lax.broadcasted_iota(jnp.int32, (span, 2 * span), 0)
    kj = lax.broadcasted_iota(jnp.int32, (span, 2 * span), 1)
    dist = qi + span - kj
    valid = (dist >= 0) & (dist <= span) & ((kj >= span) | (n > 0))
    lane = lax.broadcasted_iota(jnp.int32, (span, HEAD_DIM), 1)
    lse_tile = jnp.zeros((span, HEAD_DIM), F32)
    for h in range(heads):
        cols = slice(h * HEAD_DIM, (h + 1) * HEAD_DIM)
        q = q_ref[:, cols]
        k = jnp.concatenate([kp_ref[:, cols], kc_ref[:, cols]], axis=0)
        v = jnp.concatenate([vp_ref[:, cols], vc_ref[:, cols]], axis=0)
        s = lax.dot_general(q, k, (((1,), (1,)), ((), ())), preferred_element_type=F32) * scale
        s = jnp.where(valid, s, -jnp.inf)
        m = jnp.max(s, axis=-1, keepdims=True)
        e = jnp.exp(s - m)
        l = jnp.sum(e, axis=-1, keepdims=True)
        o = jnp.dot(e.astype(BF16), v, preferred_element_type=F32) / l
        o_ref[:, cols] = o.astype(o_ref.dtype)
        lse_tile = jnp.where(lane == h, m + jnp.log(l), lse_tile)
    lse_ref[...] = lse_tile


def banded_attention(q, k, v, *, heads=ATTN_HEADS, col_blocks=(0, 0, 0)):
    d, length, _ = q.shape
    width = heads * HEAD_DIM
    span = ATTN_SPAN
    nb = length // span

    def cur_spec(cb):
        return pl.BlockSpec((None, span, width), lambda r, n: (r, n, cb))

    def prev_spec(cb):
        return pl.BlockSpec((None, span, width), lambda r, n: (r, jnp.maximum(n - 1, 0), cb))

    qc, kc, vc = col_blocks
    return pl.pallas_call(
        functools.partial(_attn_kernel, heads=heads, scale=HEAD_DIM ** -0.5),
        grid=(d, nb),
        in_specs=[cur_spec(qc), prev_spec(kc), cur_spec(kc), prev_spec(vc), cur_spec(vc)],
        out_specs=[pl.BlockSpec((None, span, width), lambda r, n: (r, n, 0)),
                   pl.BlockSpec((None, span, HEAD_DIM), lambda r, n: (r, n, 0))],
        out_shape=[jax.ShapeDtypeStruct((d, length, width), F32),
                   jax.ShapeDtypeStruct((d, length, HEAD_DIM), F32)],
        compiler_params=_params(2),
        name="banded_attention",
    )(q, k, k, v, v)


def _attn_combine_kernel(o0_ref, o1_ref, o2_ref, l0_ref, l1_ref, l2_ref, out_ref, *, heads):
    l0, l1, l2 = l0_ref[...], l1_ref[...], l2_ref[...]
    m = jnp.maximum(jnp.maximum(l0, l1), l2)
    e0, e1, e2 = jnp.exp(l0 - m), jnp.exp(l1 - m), jnp.exp(l2 - m)
    den = e0 + e1 + e2
    w0, w1, w2 = e0 / den, e1 / den, e2 / den
    for h in range(heads):
        cols = slice(h * HEAD_DIM, (h + 1) * HEAD_DIM)
        acc = (w0[:, h:h + 1] * o0_ref[:, cols] + w1[:, h:h + 1] * o1_ref[:, cols]
               + w2[:, h:h + 1] * o2_ref[:, cols])
        out_ref[:, cols] = acc.astype(out_ref.dtype)


def attn_combine(outs, lses, *, tm=256):
    m, width = outs[0].shape
    heads = width // HEAD_DIM
    ospec = pl.BlockSpec((tm, width), lambda i: (i, 0))
    lspec = pl.BlockSpec((tm, HEAD_DIM), lambda i: (i, 0))
    return pl.pallas_call(
        functools.partial(_attn_combine_kernel, heads=heads),
        grid=(m // tm,),
        in_specs=[ospec] * 3 + [lspec] * 3,
        out_specs=ospec,
        out_shape=jax.ShapeDtypeStruct((m, width), BF16),
        compiler_params=_params(1),
        name="attn_combine",
    )(*outs, *lses)


def _softplus(x):
    return jnp.maximum(x, 0.0) + jnp.log(1.0 + jnp.exp(-jnp.abs(x)))


def _bdot(a, b, dims):
    return lax.dot_general(a, b, (dims, ((0,), (0,))), preferred_element_type=F32)


def _bmm(a, b):
    return _bdot(a, b, ((2,), (1,)))


def _split_bf16(x):
    hi = x.astype(BF16)
    lo = (x - hi.astype(F32)).astype(BF16)
    return hi, lo


def _bmm_3pass(a, b):
    a_hi, a_lo = _split_bf16(a)
    b_hi, b_lo = _split_bf16(b)
    return _bmm(a_hi, b_hi) + _bmm(a_hi, b_lo) + _bmm(a_lo, b_hi)


def _unit_lower_inverse_batched(a):
    c = a.shape[-1]
    eye = (lax.broadcasted_iota(jnp.int32, (c, c), 0)
           == lax.broadcasted_iota(jnp.int32, (c, c), 1)).astype(F32)
    p = eye - a
    x = a
    for _ in range(int(math.log2(c)) - 1):
        x = _bmm_3pass(x, x)
        p = p + _bmm_3pass(p, x)
    return p


def _dn_prep_kernel(q_ref, k_ref, v_ref, ba_ref, bat_ref, prow_ref, pcol_ref,
                    u_ref, w_ref, qg_ref, kd_ref, intra_ref, eg_ref, *, heads, chunks):
    c_len = DN_CHUNK

    ba = ba_ref[...]
    beta_col = _sigmoid(ba)
    g_col = -jnp.exp(prow_ref[0:1, :]) * _softplus(ba + prow_ref[1:2, :])
    row_in_chunk = lax.broadcasted_iota(jnp.int32, g_col.shape, 0) % c_len
    shift = 1
    while shift < c_len:
        g_col = g_col + jnp.where(row_in_chunk >= shift, pltpu.roll(g_col, shift, 0), 0.0)
        shift *= 2
    bat = bat_ref[...]
    beta_row = _sigmoid(bat)
    g_row = -jnp.exp(pcol_ref[:, 0:1])[None] * _softplus(bat + pcol_ref[:, 1:2][None])
    lane_in_chunk = lax.broadcasted_iota(jnp.int32, g_row.shape, 2)
    shift = 1
    while shift < c_len:
        g_row = g_row + jnp.where(lane_in_chunk >= shift, pltpu.roll(g_row, shift, 2), 0.0)
        shift *= 2
    eg_ref[...] = jnp.exp(g_row)

    ri = lax.broadcasted_iota(jnp.int32, (c_len, c_len), 0)
    ci = lax.broadcasted_iota(jnp.int32, (c_len, c_len), 1)
    lower = ri >= ci
    strict = ri > ci

    for c in range(chunks):
        rs = slice(c * c_len, (c + 1) * c_len)

        def per_head(x_ref):
            return jnp.stack([x_ref[rs, h * HEAD_DIM:(h + 1) * HEAD_DIM] for h in range(heads)])

        q = per_head(q_ref).astype(F32) * HEAD_DIM ** -0.5
        k16 = per_head(k_ref)
        k = k16.astype(F32)
        v16 = per_head(v_ref)
        gcol = jnp.stack([g_col[rs, heads + h:heads + h + 1] for h in range(heads)])
        bcol = jnp.stack([beta_col[rs, h:h + 1] for h in range(heads)])
        grow = jnp.stack([g_row[c, heads + h:heads + h + 1, :] for h in range(heads)])
        brow = jnp.stack([beta_row[c, h:h + 1, :] for h in range(heads)])
        decay = jnp.exp(jnp.where(lower, gcol - grow, -jnp.inf))
        q16 = q.astype(BF16)
        qk = _bdot(jnp.concatenate([q16, k16], axis=1), k16, ((2,), (2,)))
        intra = qk[:, :c_len] * decay
        a = jnp.where(strict, qk[:, c_len:] * decay * bcol, 0.0)
        t_inv = _unit_lower_inverse_batched(a)
        u = _bmm((t_inv * brow).astype(BF16), v16)
        w = _bmm((t_inv * (brow * jnp.exp(grow))).astype(BF16), k16)
        g_last = grow[:, :, c_len - 1:c_len]
        u_ref[:, rs, :] = u
        w_ref[:, rs, :] = w.astype(w_ref.dtype)
        qg_ref[:, rs, :] = (q * jnp.exp(gcol)).astype(qg_ref.dtype)
        kd_ref[:, rs, :] = (k * jnp.exp(g_last - gcol)).astype(kd_ref.dtype)
        intra_ref[:, rs, :] = intra.astype(intra_ref.dtype)


def _dn_scan_kernel(egl_ref, u_ref, w_ref, qg_ref, kd_ref, intra_ref, z_ref, gn_ref, o_ref,
                    state_ref, *, heads, chunks):
    tb = pl.program_id(0)
    c_len = DN_CHUNK

    @pl.when(tb == 0)
    def _():
        state_ref[...] = jnp.zeros_like(state_ref)

    gn = gn_ref[...]
    for c in range(chunks):
        rs = slice(c * c_len, (c + 1) * c_len)
        s = state_ref[...]
        lhs = jnp.concatenate([w_ref[:, rs, :], qg_ref[:, rs, :]], axis=1)
        r = _bmm(lhs, s.astype(BF16))
        v_new = u_ref[:, rs, :] - r[:, :c_len]
        vn16 = v_new.astype(BF16)
        o = r[:, c_len:] + _bmm(intra_ref[:, rs, :], vn16)
        upd = _bdot(kd_ref[:, rs, :], vn16, ((1,), (1,)))
        base = (tb * chunks + c) * heads
        for h in range(heads):
            state_ref[h] = s[h] * egl_ref[base + h] + upd[h]
        o = o * lax.rsqrt(jnp.mean(o * o, axis=-1, keepdims=True) + EPS) * gn
        for h in range(heads):
            cs = slice(h * HEAD_DIM, (h + 1) * HEAD_DIM)
            zz = z_ref[rs, cs].astype(F32)
            o_ref[rs, cs] = (o[h] * (zz * _sigmoid(zz))).astype(o_ref.dtype)


def gated_deltanet2(qk, v, ba, a_log, dt_bias, z, dn_norm, *, prep_chunks=2, scan_chunks=4):
    s_len, width = z.shape
    heads = width // HEAD_DIM
    c_len = DN_CHUNK
    n_chunks = s_len // c_len
    bat = ba[:, :2 * heads].reshape(n_chunks, c_len, 2 * heads).transpose(0, 2, 1)
    prow = jnp.zeros((2, HEAD_DIM), F32)
    prow = prow.at[0, heads:2 * heads].set(a_log.astype(F32))
    prow = prow.at[1, heads:2 * heads].set(dt_bias.astype(F32))
    pcol = jnp.zeros((2 * heads, 2), F32)
    pcol = pcol.at[heads:, 0].set(a_log.astype(F32)).at[heads:, 1].set(dt_bias.astype(F32))

    rows = prep_chunks * c_len

    def main_spec(i):
        return pl.BlockSpec((rows, width), lambda t: (t, i))

    def prep_out(last):
        return pl.BlockSpec((heads, rows, last), lambda t: (0, t, 0))

    u, w, qg, kd, intra, eg = pl.pallas_call(
        functools.partial(_dn_prep_kernel, heads=heads, chunks=prep_chunks),
        grid=(s_len // rows,),
        in_specs=[main_spec(0), main_spec(1), main_spec(0),
                  pl.BlockSpec((rows, HEAD_DIM), lambda t: (t, 0)),
                  pl.BlockSpec((prep_chunks, 2 * heads, c_len), lambda t: (t, 0, 0)),
                  pl.BlockSpec((2, HEAD_DIM), lambda t: (0, 0)),
                  pl.BlockSpec((2 * heads, 2), lambda t: (0, 0))],
        out_specs=[prep_out(HEAD_DIM), prep_out(HEAD_DIM), prep_out(HEAD_DIM),
                   prep_out(HEAD_DIM), prep_out(c_len),
                   pl.BlockSpec((prep_chunks, 2 * heads, c_len), lambda t: (t, 0, 0))],
        out_shape=[jax.ShapeDtypeStruct((heads, s_len, HEAD_DIM), F32),
                   jax.ShapeDtypeStruct((heads, s_len, HEAD_DIM), BF16),
                   jax.ShapeDtypeStruct((heads, s_len, HEAD_DIM), BF16),
                   jax.ShapeDtypeStruct((heads, s_len, HEAD_DIM), BF16),
                   jax.ShapeDtypeStruct((heads, s_len, c_len), BF16),
                   jax.ShapeDtypeStruct((n_chunks, 2 * heads, c_len), F32)],
        compiler_params=_params(1),
        name="deltanet_prep",
    )(qk, qk, v, ba, bat, prow, pcol)

    egl = eg[:, heads:, c_len - 1].reshape(n_chunks * heads)

    scan_rows = scan_chunks * c_len

    def head_major(last):
        return pl.BlockSpec((heads, scan_rows, last), lambda t: (0, t, 0))

    nat = pl.BlockSpec((scan_rows, width), lambda t: (t, 0))
    return pl.pallas_call(
        functools.partial(_dn_scan_kernel, heads=heads, chunks=scan_chunks),
        grid=(s_len // scan_rows,),
        in_specs=[pl.BlockSpec(memory_space=pltpu.SMEM),
                  head_major(HEAD_DIM), head_major(HEAD_DIM), head_major(HEAD_DIM),
                  head_major(HEAD_DIM), head_major(c_len), nat,
                  pl.BlockSpec((1, HEAD_DIM), lambda t: (0, 0))],
        out_specs=nat,
        out_shape=jax.ShapeDtypeStruct((s_len, width), BF16),
        scratch_shapes=[pltpu.VMEM((heads, HEAD_DIM, HEAD_DIM), F32)],
        compiler_params=_params(1),
        name="deltanet_scan",
    )(egl, u, w, qg, kd, intra, z, dn_norm.reshape(1, HEAD_DIM).astype(F32))


def _proj_norm_kernel(p_ref, w_ref, g_ref, o_ref, wcast_ref):
    @pl.when(pl.program_id(0) == 0)
    def _():
        wcast_ref[...] = w_ref[...].astype(BF16)

    y = jnp.dot(p_ref[...].astype(BF16), wcast_ref[...], preferred_element_type=F32)
    ms = jnp.mean(y * y, axis=-1, keepdims=True)
    o_ref[...] = (y * lax.rsqrt(ms + EPS) * g_ref[...]).astype(o_ref.dtype)


def proj_norm(p, w, g, out_dtype, *, tm=512):
    m, kdim = p.shape
    n = w.shape[1]
    return pl.pallas_call(
        _proj_norm_kernel,
        grid=(m // tm,),
        in_specs=[pl.BlockSpec((tm, kdim), lambda i: (i, 0)),
                  pl.BlockSpec((kdim, n), lambda i: (0, 0)),
                  pl.BlockSpec((1, n), lambda i: (0, 0))],
        out_specs=pl.BlockSpec((tm, n), lambda i: (i, 0)),
        out_shape=jax.ShapeDtypeStruct((m, n), out_dtype),
        scratch_shapes=[pltpu.VMEM((kdim, n), BF16)],
        compiler_params=_params(1),
        name="proj_norm",
    )(p, w, g.reshape(1, n).astype(F32))


def _layer(x, p, w_in, conv_w, a_log, dt_bias, dn_norm, w_attn_up, w_dn_up, w_out, w_mlp_up,
           w_mlp_down, w_ple_gate, w_ple_proj, norm_mix, norm_mlp, norm_ple, ple_post_norm):
    s_len, d_model = x.shape
    attn_w = ATTN_HEADS * HEAD_DIM
    dn_w = DN_HEADS * HEAD_DIM
    n_groups = len(ATTN_DILATIONS)
    c_qkv_b = 3 * n_groups * attn_w
    c_z = c_qkv_b + 3 * dn_w
    c_ba = c_z + dn_w
    c_gate = c_ba + 2 * DN_HEADS
    mm = matmul_ws

    w_in_t = w_in.T
    h = rmsnorm(x, norm_mix, BF16)
    qkv_a = mm(h, w_in_t, b_is_nk=True, col0=0, n=c_qkv_b, out_dtype=BF16, name="proj_attn")
    qk_b = mm(h, w_in_t, b_is_nk=True, col0=c_qkv_b, n=2 * dn_w, out_dtype=BF16,
              epilogue="conv_silu_l2norm", extra=((conv_w, 0),), name="proj_dn_qk")
    v_b = mm(h, w_in_t, b_is_nk=True, col0=c_qkv_b + 2 * dn_w, n=dn_w, out_dtype=BF16,
             epilogue="conv_silu", extra=((conv_w, 2 * dn_w),), name="proj_dn_v")
    z_b = mm(h, w_in_t, b_is_nk=True, col0=c_z, n=dn_w, out_dtype=BF16, name="proj_z")
    ba = mm(h, w_in_t, b_is_nk=True, col0=c_ba, n=LANES, tn=LANES, out_dtype=F32, name="proj_ba")
    gates = mm(h, w_in_t, b_is_nk=True, col0=c_gate, n=2 * d_model, out_dtype=BF16,
               name="proj_gates")

    outs, lses = [], []
    for gi, dil in enumerate(ATTN_DILATIONS):
        if dil == 1:
            arr = qkv_a.reshape(1, s_len, c_qkv_b)
            o_g, lse_g = banded_attention(
                arr, arr, arr, col_blocks=(gi, n_groups + gi, 2 * n_groups + gi))
        else:
            def sub(t):
                cols = qkv_a[:, (t * n_groups + gi) * attn_w:(t * n_groups + gi + 1) * attn_w]
                return cols.reshape(s_len // dil, dil, attn_w).transpose(1, 0, 2)
            o_g, lse_g = banded_attention(sub(0), sub(1), sub(2))
        outs.append(o_g.transpose(1, 0, 2).reshape(s_len, attn_w))
        lses.append(lse_g.transpose(1, 0, 2).reshape(s_len, HEAD_DIM))
    o_a = attn_combine(outs, lses)

    o_b = gated_deltanet2(qk_b, v_b, ba, a_log, dt_bias, z_b, dn_norm)

    t_a = mm(o_a, w_attn_up, out_dtype=BF16, epilogue="gate", extra=((gates, 0),), name="attn_up")
    merged = mm(o_b, w_dn_up, out_dtype=BF16, epilogue="gate_add",
                extra=((gates, d_model), (t_a, 0)), name="dn_up")
    x = mm(merged, w_out, out_dtype=F32, epilogue="residual", extra=((x, 0),), name="out_proj")

    h = rmsnorm(x, norm_mlp, BF16)
    u = mm(h, w_mlp_up, out_dtype=BF16, epilogue="relu2", name="mlp_up")
    x = mm(u, w_mlp_down, out_dtype=F32, epilogue="residual", extra=((x, 0),), tk=2048,
           name="mlp_down")

    h = rmsnorm(x, norm_ple, BF16)
    pp = proj_norm(p, w_ple_proj, ple_post_norm, F32)
    x = mm(h, w_ple_gate, out_dtype=F32, epilogue="ple", extra=((x, 0), (pp, 0)), name="ple_gate")
    return x


def kernel(x, p, w_in, conv_w, dn_a_log, dn_dt_bias, dn_norm, w_attn_up, w_dn_up, w_out,
           w_mlp_up, w_mlp_down, w_ple_gate, w_ple_proj, norm_mix, norm_mlp, norm_ple,
           ple_post_norm, final_norm):
    b, s_len, d_model = x.shape
    assert b == 1
    depth = w_in.shape[0]
    xs = x.reshape(s_len, d_model)
    for i in range(depth):
        xs = _layer(xs, p[i, 0], w_in[i], conv_w[i], dn_a_log[i], dn_dt_bias[i], dn_norm[i],
                    w_attn_up[i], w_dn_up[i], w_out[i], w_mlp_up[i], w_mlp_down[i],
                    w_ple_gate[i], w_ple_proj[i], norm_mix[i], norm_mlp[i], norm_ple[i],
                    ple_post_norm[i])
    out = rmsnorm(xs, final_norm, x.dtype)
    return out.reshape(b, s_len, d_model)
```

```python
import functools
import math

import jax
import jax.numpy as jnp
from jax import lax
from jax.experimental import pallas as pl
from jax.experimental.pallas import tpu as pltpu

F32 = jnp.float32
BF16 = jnp.bfloat16

EPS = 1e-6
LANES = 128
SUBLANES = 8
HEAD_DIM = 128
ATTN_SPAN = 128
ATTN_DILATIONS = (1, 4, 16)
ATTN_HEADS = 8
DN_HEADS = 16
DN_CHUNK = 64
CONV_K = 4

VMEM_LIMIT_BYTES = 56 * 1024 * 1024


def _params(n_axes):
    return pltpu.CompilerParams(
        dimension_semantics=("arbitrary",) * n_axes,
        vmem_limit_bytes=VMEM_LIMIT_BYTES,
    )


def _rmsnorm_kernel(x_ref, g_ref, o_ref):
    x = x_ref[...].astype(F32)
    ms = jnp.mean(x * x, axis=-1, keepdims=True)
    o_ref[...] = (x * lax.rsqrt(ms + EPS) * g_ref[...]).astype(o_ref.dtype)


def rmsnorm(x, g, out_dtype, *, tm=256):
    m, d = x.shape
    return pl.pallas_call(
        _rmsnorm_kernel,
        grid=(m // tm,),
        in_specs=[pl.BlockSpec((tm, d), lambda i: (i, 0)),
                  pl.BlockSpec((1, d), lambda i: (0, 0))],
        out_specs=pl.BlockSpec((tm, d), lambda i: (i, 0)),
        out_shape=jax.ShapeDtypeStruct((m, d), out_dtype),
        compiler_params=_params(1),
        name="rmsnorm",
    )(x, g.reshape(1, d).astype(F32))


def _sigmoid(x):
    return 1.0 / (1.0 + jnp.exp(-x))


def _mm_kernel(*refs, nk, n_tiles, tm, epilogue, n_extra, b_is_nk):
    a_ref, bchunk_ref = refs[0], refs[1]
    extra = refs[2:2 + n_extra]
    o_ref = refs[2 + n_extra]
    bcast_ref = refs[3 + n_extra]
    tail_refs = refs[4 + n_extra:]
    acc_ref = tail_refs[0] if nk > 1 else None
    carry_ref = tail_refs[-1] if epilogue.startswith("conv_silu") else None
    t = pl.program_id(0)
    i = pl.program_id(1)
    chunk_rows = bchunk_ref.shape[0]

    @pl.when(t < n_tiles)
    def _():
        rows = pl.ds(pl.multiple_of(i * chunk_rows, chunk_rows), chunk_rows)
        bcast_ref[t % 2, rows, :] = bchunk_ref[...].astype(BF16)

    @pl.when(t > 0)
    def _():
        _mm_compute(a_ref, bcast_ref, extra, o_ref, acc_ref, carry_ref, t - 1, i, nk=nk, tm=tm,
                    epilogue=epilogue, b_is_nk=b_is_nk)


def _mm_compute(a_ref, bcast_ref, extra, o_ref, acc_ref, carry_ref, tile, i, *, nk, tm, epilogue,
                b_is_nk):
    k = tile % nk
    w = bcast_ref[tile % 2]
    if b_is_nk:
        part = lax.dot_general(a_ref[...], w, (((1,), (1,)), ((), ())),
                               preferred_element_type=F32)
    else:
        part = jnp.dot(a_ref[...], w, preferred_element_type=F32)

    def finish(acc):
        if epilogue == "plain":
            val = acc
        elif epilogue == "relu2":
            r = jnp.maximum(acc, 0.0)
            val = r * r
        elif epilogue == "residual":
            val = extra[0][...].astype(F32) + acc
        elif epilogue == "gate":
            val = _sigmoid(extra[0][...].astype(F32)) * acc
        elif epilogue == "gate_add":
            val = extra[1][...].astype(F32) + _sigmoid(extra[0][...].astype(F32)) * acc
        elif epilogue == "ple":
            val = extra[0][...].astype(F32) + _sigmoid(acc) * extra[1][...].astype(F32)
        elif epilogue.startswith("conv_silu"):
            prev = jnp.where(i == 0, 0.0, carry_ref[...])
            xx = jnp.concatenate([prev, acc], axis=0)
            cw = extra[0][...]
            base = prev.shape[0] - (CONV_K - 1)
            y = xx[base:base + tm] * cw[0:1]
            for tap in range(1, CONV_K):
                y = y + xx[base + tap:base + tap + tm] * cw[tap:tap + 1]
            carry_ref[...] = acc[tm - prev.shape[0]:, :]
            y = y * _sigmoid(y)
            if epilogue == "conv_silu_l2norm":
                heads_in_tile = y.shape[1] // HEAD_DIM
                segs = []
                for hh in range(heads_in_tile):
                    seg = y[:, hh * HEAD_DIM:(hh + 1) * HEAD_DIM]
                    segs.append(seg * lax.rsqrt(jnp.sum(seg * seg, axis=-1, keepdims=True) + EPS))
                y = jnp.concatenate(segs, axis=1)
            val = y
        else:
            raise ValueError(epilogue)
        o_ref[...] = val.astype(o_ref.dtype)

    if nk == 1:
        finish(part)
    else:
        rows = pl.ds(pl.multiple_of(i * tm, tm), tm)

        @pl.when(k == 0)
        def _():
            acc_ref[rows, :] = part

        @pl.when(jnp.logical_and(k > 0, k < nk - 1))
        def _():
            acc_ref[rows, :] += part

        @pl.when(k == nk - 1)
        def _():
            finish(acc_ref[rows, :] + part)


def matmul_ws(a, b, *, col0=0, n=None, out_dtype=BF16, epilogue="plain", extra=(),
              tm=1024, tn=512, tk=None, b_is_nk=False, name="matmul_ws"):
    m, kdim = a.shape
    n_total = b.shape[0] if b_is_nk else b.shape[1]
    if n is None:
        n = n_total - col0
    if tk is None:
        tk = kdim
    assert m % tm == 0 and n % tn == 0 and kdim % tk == 0 and col0 + n <= n_total
    nj, nk, ni = n // tn, kdim // tk, m // tm
    n_tiles = nj * nk

    def tile_jk(t):
        tile = jnp.maximum(t - 1, 0)
        return tile // nk, tile % nk

    def row_block(t, i):
        _, k = tile_jk(t)
        return jnp.where(jnp.logical_and(t > 0, k == nk - 1), i, 0)

    def a_map(t, i):
        return (jnp.where(t > 0, i, 0), tile_jk(t)[1])

    def out_map(t, i):
        return (row_block(t, i), tile_jk(t)[0])

    def staged(t, i):
        tile = jnp.minimum(t, n_tiles - 1)
        return tile // nk, tile % nk, jnp.where(t == n_tiles, ni - 1, i)

    if b_is_nk:
        assert col0 % SUBLANES == 0 and tn % (ni * SUBLANES) == 0
        cn = tn // ni

        def b_map(t, i):
            j, k, c = staged(t, i)
            return (pl.multiple_of(col0 + j * tn + c * cn, SUBLANES), pl.multiple_of(k * tk, LANES))

        b_spec = pl.BlockSpec((pl.Element(cn), pl.Element(tk)), b_map)
        bcast_shape = (2, tn, tk)
    else:
        assert col0 % tn == 0 and tk % (ni * SUBLANES) == 0

        def b_map(t, i):
            j, k, c = staged(t, i)
            return (k * ni + c, j + col0 // tn)

        b_spec = pl.BlockSpec((tk // ni, tn), b_map)
        bcast_shape = (2, tk, tn)
    in_specs = [pl.BlockSpec((tm, tk), a_map), b_spec]
    conv = epilogue.startswith("conv_silu")
    for idx, (_, ecol0) in enumerate(extra):
        assert ecol0 % tn == 0
        if conv and idx == 0:
            in_specs.append(pl.BlockSpec(
                (CONV_K, tn),
                functools.partial(lambda t, i, e0: (0, tile_jk(t)[0] + e0), e0=ecol0 // tn)))
        else:
            in_specs.append(pl.BlockSpec(
                (tm, tn),
                functools.partial(lambda t, i, e0: (row_block(t, i), tile_jk(t)[0] + e0),
                                  e0=ecol0 // tn)))
    scratch = [pltpu.VMEM(bcast_shape, BF16)]
    if nk > 1:
        scratch.append(pltpu.VMEM((m, tn), F32))
    if conv:
        assert nk == 1
        scratch.append(pltpu.VMEM((SUBLANES, tn), F32))
    return pl.pallas_call(
        functools.partial(_mm_kernel, nk=nk, n_tiles=n_tiles, tm=tm, epilogue=epilogue,
                          n_extra=len(extra), b_is_nk=b_is_nk),
        grid=(n_tiles + 1, ni),
        in_specs=in_specs,
        out_specs=pl.BlockSpec((tm, tn), out_map),
        out_shape=jax.ShapeDtypeStruct((m, n), out_dtype),
        scratch_shapes=scratch,
        compiler_params=_params(2),
        name=name,
    )(a, b, *[arr for arr, _ in extra])


def _attn_kernel(q_ref, kp_ref, kc_ref, vp_ref, vc_ref, o_ref, lse_ref, *, heads, scale):
    n = pl.program_id(1)
    span = q_ref.shape[0]
    qi = lax.broadcasted_iota(jnp.int32, (span, 2 * span), 0)
    kj = lax.broadcasted_iota(jnp.int32, (span, 2 * span), 1)
    dist = qi + span - kj
    valid = (dist >= 0) & (dist <= span) & ((kj >= span) | (n > 0))
    lane = lax.broadcasted_iota(jnp.int32, (span, HEAD_DIM), 1)
    lse_tile = jnp.zeros((span, HEAD_DIM), F32)
    for h in range(heads):
        cols = slice(h * HEAD_DIM, (h + 1) * HEAD_DIM)
        q = q_ref[:, cols]
        k = jnp.concatenate([kp_ref[:, cols], kc_ref[:, cols]], axis=0)
        v = jnp.concatenate([vp_ref[:, cols], vc_ref[:, cols]], axis=0)
        s = lax.dot_general(q, k, (((1,), (1,)), ((), ())), preferred_element_type=F32) * scale
        s = jnp.where(valid, s, -jnp.inf)
        m = jnp.max(s, axis=-1, keepdims=True)
        e = jnp.exp(s - m)
        l = jnp.sum(e, axis=-1, keepdims=True)
        o = jnp.dot(e.astype(BF16), v, preferred_element_type=F32) / l
        o_ref[:, cols] = o.astype(o_ref.dtype)
        lse_tile = jnp.where(lane == h, m + jnp.log(l), lse_tile)
    lse_ref[...] = lse_tile


def banded_attention(q, k, v, *, heads=ATTN_HEADS, col_blocks=(0, 0, 0)):
    d, length, _ = q.shape
    width = heads * HEAD_DIM
    span = ATTN_SPAN
    nb = length // span

    def cur_spec(cb):
        return pl.BlockSpec((None, span, width), lambda r, n: (r, n, cb))

    def prev_spec(cb):
        return pl.BlockSpec((None, span, width), lambda r, n: (r, jnp.maximum(n - 1, 0), cb))

    qc, kc, vc = col_blocks
    return pl.pallas_call(
        functools.partial(_attn_kernel, heads=heads, scale=HEAD_DIM ** -0.5),
        grid=(d, nb),
        in_specs=[cur_spec(qc), prev_spec(kc), cur_spec(kc), prev_spec(vc), cur_spec(vc)],
        out_specs=[pl.BlockSpec((None, span, width), lambda r, n: (r, n, 0)),
                   pl.BlockSpec((None, span, HEAD_DIM), lambda r, n: (r, n, 0))],
        out_shape=[jax.ShapeDtypeStruct((d, length, width), BF16),
                   jax.ShapeDtypeStruct((d, length, HEAD_DIM), F32)],
        compiler_params=_params(2),
        name="banded_attention",
    )(q, k, k, v, v)


def _attn_combine_kernel(o0_ref, o1_ref, o2_ref, l0_ref, l1_ref, l2_ref, out_ref, *, heads):
    l0, l1, l2 = l0_ref[...], l1_ref[...], l2_ref[...]
    m = jnp.maximum(jnp.maximum(l0, l1), l2)
    e0, e1, e2 = jnp.exp(l0 - m), jnp.exp(l1 - m), jnp.exp(l2 - m)
    den = e0 + e1 + e2
    w0, w1, w2 = e0 / den, e1 / den, e2 / den
    for h in range(heads):
        cols = slice(h * HEAD_DIM, (h + 1) * HEAD_DIM)
        acc = (w0[:, h:h + 1] * o0_ref[:, cols] + w1[:, h:h + 1] * o1_ref[:, cols]
               + w2[:, h:h + 1] * o2_ref[:, cols])
        out_ref[:, cols] = acc.astype(out_ref.dtype)


def attn_combine(outs, lses, *, tm=256):
    m, width = outs[0].shape
    heads = width // HEAD_DIM
    ospec = pl.BlockSpec((tm, width), lambda i: (i, 0))
    lspec = pl.BlockSpec((tm, HEAD_DIM), lambda i: (i, 0))
    return pl.pallas_call(
        functools.partial(_attn_combine_kernel, heads=heads),
        grid=(m // tm,),
        in_specs=[ospec] * 3 + [lspec] * 3,
        out_specs=ospec,
        out_shape=jax.ShapeDtypeStruct((m, width), BF16),
        compiler_params=_params(1),
        name="attn_combine",
    )(*outs, *lses)


def _softplus(x):
    return jnp.maximum(x, 0.0) + jnp.log(1.0 + jnp.exp(-jnp.abs(x)))


def _bdot(a, b, dims):
    return lax.dot_general(a, b, (dims, ((0,), (0,))), preferred_element_type=F32)


def _bmm(a, b):
    return _bdot(a, b, ((2,), (1,)))


def _split_bf16(x):
    hi = x.astype(BF16)
    lo = (x - hi.astype(F32)).astype(BF16)
    return hi, lo


def _bmm_split(a, b):
    a_hi, a_lo = _split_bf16(a)
    b_hi, b_lo = _split_bf16(b)
    return _bmm(a_hi, b_hi) + _bmm(a_hi, b_lo) + _bmm(a_lo, b_hi)


def _unit_lower_inverse_batched(a):
    c = a.shape[-1]
    eye = (lax.broadcasted_iota(jnp.int32, (c, c), 0)
           == lax.broadcasted_iota(jnp.int32, (c, c), 1)).astype(F32)
    p = eye - a
    x = a
    for _ in range(int(math.log2(c)) - 1):
        x = _bmm_split(x, x)
        p = p + _bmm_split(p, x)
    return p


def _dn_prep_kernel(q_ref, k_ref, v_ref, ba_ref, bat_ref, prow_ref, pcol_ref,
                    u_ref, w_ref, qg_ref, kd_ref, intra_ref, eg_ref, *, heads, chunks):
    c_len = DN_CHUNK

    ba = ba_ref[...]
    beta_col = _sigmoid(ba)
    g_col = -jnp.exp(prow_ref[0:1, :]) * _softplus(ba + prow_ref[1:2, :])
    row_in_chunk = lax.broadcasted_iota(jnp.int32, g_col.shape, 0) % c_len
    shift = 1
    while shift < c_len:
        g_col = g_col + jnp.where(row_in_chunk >= shift, pltpu.roll(g_col, shift, 0), 0.0)
        shift *= 2
    bat = bat_ref[...]
    beta_row = _sigmoid(bat)
    g_row = -jnp.exp(pcol_ref[:, 0:1])[None] * _softplus(bat + pcol_ref[:, 1:2][None])
    lane_in_chunk = lax.broadcasted_iota(jnp.int32, g_row.shape, 2)
    shift = 1
    while shift < c_len:
        g_row = g_row + jnp.where(lane_in_chunk >= shift, pltpu.roll(g_row, shift, 2), 0.0)
        shift *= 2
    eg_ref[...] = jnp.exp(g_row)

    ri = lax.broadcasted_iota(jnp.int32, (c_len, c_len), 0)
    ci = lax.broadcasted_iota(jnp.int32, (c_len, c_len), 1)
    lower = ri >= ci
    strict = ri > ci

    for c in range(chunks):
        rs = slice(c * c_len, (c + 1) * c_len)

        def per_head(x_ref):
            return jnp.stack([x_ref[rs, h * HEAD_DIM:(h + 1) * HEAD_DIM] for h in range(heads)])

        q = per_head(q_ref).astype(F32) * HEAD_DIM ** -0.5
        k16 = per_head(k_ref)
        k = k16.astype(F32)
        v16 = per_head(v_ref)
        gcol = jnp.stack([g_col[rs, heads + h:heads + h + 1] for h in range(heads)])
        bcol = jnp.stack([beta_col[rs, h:h + 1] for h in range(heads)])
        grow = jnp.stack([g_row[c, heads + h:heads + h + 1, :] for h in range(heads)])
        brow = jnp.stack([beta_row[c, h:h + 1, :] for h in range(heads)])
        decay = jnp.exp(jnp.where(lower, gcol - grow, -jnp.inf))
        q16 = q.astype(BF16)
        qk = _bdot(jnp.concatenate([q16, k16], axis=1), k16, ((2,), (2,)))
        intra = qk[:, :c_len] * decay
        a = jnp.where(strict, qk[:, c_len:] * decay * bcol, 0.0)
        t_inv = _unit_lower_inverse_batched(a)
        u = _bmm((t_inv * brow).astype(BF16), v16)
        w = _bmm((t_inv * (brow * jnp.exp(grow))).astype(BF16), k16)
        g_last = grow[:, :, c_len - 1:c_len]
        u_ref[:, rs, :] = u
        w_ref[:, rs, :] = w.astype(w_ref.dtype)
        qg_ref[:, rs, :] = (q * jnp.exp(gcol)).astype(qg_ref.dtype)
        kd_ref[:, rs, :] = (k * jnp.exp(g_last - gcol)).astype(kd_ref.dtype)
        intra_ref[:, rs, :] = intra.astype(intra_ref.dtype)


def _dn_scan_kernel(egl_ref, u_ref, w_ref, qg_ref, kd_ref, intra_ref, z_ref, gn_ref, o_ref,
                    state_ref, *, heads, chunks):
    tb = pl.program_id(0)
    c_len = DN_CHUNK

    @pl.when(tb == 0)
    def _():
        state_ref[...] = jnp.zeros_like(state_ref)

    gn = gn_ref[...]
    for c in range(chunks):
        rs = slice(c * c_len, (c + 1) * c_len)
        s = state_ref[...]
        lhs = jnp.concatenate([w_ref[:, rs, :], qg_ref[:, rs, :]], axis=1)
        r = _bmm(lhs, s.astype(BF16))
        v_new = u_ref[:, rs, :] - r[:, :c_len]
        vn16 = v_new.astype(BF16)
        o = r[:, c_len:] + _bmm(intra_ref[:, rs, :], vn16)
        upd = _bdot(kd_ref[:, rs, :], vn16, ((1,), (1,)))
        base = (tb * chunks + c) * heads
        for h in range(heads):
            state_ref[h] = s[h] * egl_ref[base + h] + upd[h]
        o = o * lax.rsqrt(jnp.mean(o * o, axis=-1, keepdims=True) + EPS) * gn
        for h in range(heads):
            cs = slice(h * HEAD_DIM, (h + 1) * HEAD_DIM)
            zz = z_ref[rs, cs].astype(F32)
            o_ref[rs, cs] = (o[h] * (zz * _sigmoid(zz))).astype(o_ref.dtype)


def gated_deltanet2(qk, v, ba, a_log, dt_bias, z, dn_norm, *, prep_chunks=2, scan_chunks=4):
    s_len, width = z.shape
    heads = width // HEAD_DIM
    c_len = DN_CHUNK
    n_chunks = s_len // c_len
    bat = ba[:, :2 * heads].reshape(n_chunks, c_len, 2 * heads).transpose(0, 2, 1)
    prow = jnp.zeros((2, HEAD_DIM), F32)
    prow = prow.at[0, heads:2 * heads].set(a_log.astype(F32))
    prow = prow.at[1, heads:2 * heads].set(dt_bias.astype(F32))
    pcol = jnp.zeros((2 * heads, 2), F32)
    pcol = pcol.at[heads:, 0].set(a_log.astype(F32)).at[heads:, 1].set(dt_bias.astype(F32))

    rows = prep_chunks * c_len

    def main_spec(i):
        return pl.BlockSpec((rows, width), lambda t: (t, i))

    def prep_out(last):
        return pl.BlockSpec((heads, rows, last), lambda t: (0, t, 0))

    u, w, qg, kd, intra, eg = pl.pallas_call(
        functools.partial(_dn_prep_kernel, heads=heads, chunks=prep_chunks),
        grid=(s_len // rows,),
        in_specs=[main_spec(0), main_spec(1), main_spec(0),
                  pl.BlockSpec((rows, HEAD_DIM), lambda t: (t, 0)),
                  pl.BlockSpec((prep_chunks, 2 * heads, c_len), lambda t: (t, 0, 0)),
                  pl.BlockSpec((2, HEAD_DIM), lambda t: (0, 0)),
                  pl.BlockSpec((2 * heads, 2), lambda t: (0, 0))],
        out_specs=[prep_out(HEAD_DIM), prep_out(HEAD_DIM), prep_out(HEAD_DIM),
                   prep_out(HEAD_DIM), prep_out(c_len),
                   pl.BlockSpec((prep_chunks, 2 * heads, c_len), lambda t: (t, 0, 0))],
        out_shape=[jax.ShapeDtypeStruct((heads, s_len, HEAD_DIM), F32),
                   jax.ShapeDtypeStruct((heads, s_len, HEAD_DIM), BF16),
                   jax.ShapeDtypeStruct((heads, s_len, HEAD_DIM), BF16),
                   jax.ShapeDtypeStruct((heads, s_len, HEAD_DIM), BF16),
                   jax.ShapeDtypeStruct((heads, s_len, c_len), BF16),
                   jax.ShapeDtypeStruct((n_chunks, 2 * heads, c_len), F32)],
        compiler_params=_params(1),
        name="deltanet_prep",
    )(qk, qk, v, ba, bat, prow, pcol)

    egl = eg[:, heads:, c_len - 1].reshape(n_chunks * heads)

    scan_rows = scan_chunks * c_len

    def head_major(last):
        return pl.BlockSpec((heads, scan_rows, last), lambda t: (0, t, 0))

    nat = pl.BlockSpec((scan_rows, width), lambda t: (t, 0))
    return pl.pallas_call(
        functools.partial(_dn_scan_kernel, heads=heads, chunks=scan_chunks),
        grid=(s_len // scan_rows,),
        in_specs=[pl.BlockSpec(memory_space=pltpu.SMEM),
                  head_major(HEAD_DIM), head_major(HEAD_DIM), head_major(HEAD_DIM),
                  head_major(HEAD_DIM), head_major(c_len), nat,
                  pl.BlockSpec((1, HEAD_DIM), lambda t: (0, 0))],
        out_specs=nat,
        out_shape=jax.ShapeDtypeStruct((s_len, width), BF16),
        scratch_shapes=[pltpu.VMEM((heads, HEAD_DIM, HEAD_DIM), F32)],
        compiler_params=_params(1),
        name="deltanet_scan",
    )(egl, u, w, qg, kd, intra, z, dn_norm.reshape(1, HEAD_DIM).astype(F32))


def _proj_norm_kernel(p_ref, w_ref, g_ref, o_ref, wcast_ref):
    @pl.when(pl.program_id(0) == 0)
    def _():
        wcast_ref[...] = w_ref[...].astype(BF16)

    y = jnp.dot(p_ref[...].astype(BF16), wcast_ref[...], preferred_element_type=F32)
    ms = jnp.mean(y * y, axis=-1, keepdims=True)
    o_ref[...] = (y * lax.rsqrt(ms + EPS) * g_ref[...]).astype(o_ref.dtype)


def proj_norm(p, w, g, out_dtype, *, tm=512):
    m, kdim = p.shape
    n = w.shape[1]
    return pl.pallas_call(
        _proj_norm_kernel,
        grid=(m // tm,),
        in_specs=[pl.BlockSpec((tm, kdim), lambda i: (i, 0)),
                  pl.BlockSpec((kdim, n), lambda i: (0, 0)),
                  pl.BlockSpec((1, n), lambda i: (0, 0))],
        out_specs=pl.BlockSpec((tm, n), lambda i: (i, 0)),
        out_shape=jax.ShapeDtypeStruct((m, n), out_dtype),
        scratch_shapes=[pltpu.VMEM((kdim, n), BF16)],
        compiler_params=_params(1),
        name="proj_norm",
    )(p, w, g.reshape(1, n).astype(F32))


def _layer(x, p, w_in, conv_w, a_log, dt_bias, dn_norm, w_attn_up, w_dn_up, w_out, w_mlp_up,
           w_mlp_down, w_ple_gate, w_ple_proj, norm_mix, norm_mlp, norm_ple, ple_post_norm):
    s_len, d_model = x.shape
    attn_w = ATTN_HEADS * HEAD_DIM
    dn_w = DN_HEADS * HEAD_DIM
    n_groups = len(ATTN_DILATIONS)
    c_qkv_b = 3 * n_groups * attn_w
    c_z = c_qkv_b + 3 * dn_w
    c_ba = c_z + dn_w
    c_gate = c_ba + 2 * DN_HEADS
    mm = matmul_ws

    w_in_t = w_in.T
    h = rmsnorm(x, norm_mix, BF16)
    qkv_a = mm(h, w_in_t, b_is_nk=True, col0=0, n=c_qkv_b, tn=1024, out_dtype=BF16,
               name="proj_attn")
    qk_b = mm(h, w_in_t, b_is_nk=True, col0=c_qkv_b, n=2 * dn_w, tn=1024, out_dtype=BF16,
              epilogue="conv_silu_l2norm", extra=((conv_w, 0),), name="proj_dn_qk")
    v_b = mm(h, w_in_t, b_is_nk=True, col0=c_qkv_b + 2 * dn_w, n=dn_w, tn=1024, out_dtype=BF16,
             epilogue="conv_silu", extra=((conv_w, 2 * dn_w),), name="proj_dn_v")
    z_b = mm(h, w_in_t, b_is_nk=True, col0=c_z, n=dn_w, tn=1024, out_dtype=BF16, name="proj_z")
    ba = mm(h, w_in_t, b_is_nk=True, col0=c_ba, n=LANES, tn=LANES, out_dtype=F32, name="proj_ba")
    gates = mm(h, w_in_t, b_is_nk=True, col0=c_gate, n=2 * d_model, tn=1024, out_dtype=BF16,
               name="proj_gates")

    outs, lses = [], []
    for gi, dil in enumerate(ATTN_DILATIONS):
        if dil == 1:
            arr = qkv_a.reshape(1, s_len, c_qkv_b)
            o_g, lse_g = banded_attention(
                arr, arr, arr, col_blocks=(gi, n_groups + gi, 2 * n_groups + gi))
        else:
            def sub(t):
                cols = qkv_a[:, (t * n_groups + gi) * attn_w:(t * n_groups + gi + 1) * attn_w]
                return cols.reshape(s_len // dil, dil, attn_w).transpose(1, 0, 2)
            o_g, lse_g = banded_attention(sub(0), sub(1), sub(2))
        outs.append(o_g.transpose(1, 0, 2).reshape(s_len, attn_w))
        lses.append(lse_g.transpose(1, 0, 2).reshape(s_len, HEAD_DIM))
    o_a = attn_combine(outs, lses)

    o_b = gated_deltanet2(qk_b, v_b, ba, a_log, dt_bias, z_b, dn_norm)

    t_a = mm(o_a, w_attn_up, tn=1024, out_dtype=BF16, epilogue="gate", extra=((gates, 0),),
             name="attn_up")
    merged = mm(o_b, w_dn_up, tn=1024, out_dtype=BF16, epilogue="gate_add",
                extra=((gates, d_model), (t_a, 0)), name="dn_up")
    x = mm(merged, w_out, out_dtype=F32, epilogue="residual", extra=((x, 0),), name="out_proj")

    h = rmsnorm(x, norm_mlp, BF16)
    u = mm(h, w_mlp_up, tn=1024, out_dtype=BF16, epilogue="relu2", name="mlp_up")
    x = mm(u, w_mlp_down, out_dtype=F32, epilogue="residual", extra=((x, 0),), tk=4096,
           name="mlp_down")

    h = rmsnorm(x, norm_ple, BF16)
    pp = proj_norm(p, w_ple_proj, ple_post_norm, F32)
    x = mm(h, w_ple_gate, out_dtype=F32, epilogue="ple", extra=((x, 0), (pp, 0)), name="ple_gate")
    return x


def kernel(x, p, w_in, conv_w, dn_a_log, dn_dt_bias, dn_norm, w_attn_up, w_dn_up, w_out,
           w_mlp_up, w_mlp_down, w_ple_gate, w_ple_proj, norm_mix, norm_mlp, norm_ple,
           ple_post_norm, final_norm):
    b, s_len, d_model = x.shape
    assert b == 1
    depth = w_in.shape[0]
    xs = x.reshape(s_len, d_model)
    for i in range(depth):
        xs = _layer(xs, p[i, 0], w_in[i], conv_w[i], dn_a_log[i], dn_dt_bias[i], dn_norm[i],
                    w_attn_up[i], w_dn_up[i], w_out[i], w_mlp_up[i], w_mlp_down[i],
                    w_ple_gate[i], w_ple_proj[i], norm_mix[i], norm_mlp[i], norm_ple[i],
                    ple_post_norm[i])
    out = rmsnorm(xs, final_norm, x.dtype)
    return out.reshape(b, s_len, d_model)
```

```python
import functools
import math

import jax
import jax.numpy as jnp
from jax import lax
from jax.experimental import pallas as pl
from jax.experimental.pallas import tpu as pltpu

F32 = jnp.float32
BF16 = jnp.bfloat16

EPS = 1e-6
LANES = 128
SUBLANES = 8
MXU_WIDTH = 256
HEAD_DIM = 128
ATTN_SPAN = 128
ATTN_DILATIONS = (1, 4, 16)
ATTN_HEADS = 8
DN_HEADS = 16
DN_CHUNK = 64
CONV_K = 4

VMEM_LIMIT_BYTES = 56 * 1024 * 1024


def _params(n_axes):
    return pltpu.CompilerParams(
        dimension_semantics=("arbitrary",) * n_axes,
        vmem_limit_bytes=VMEM_LIMIT_BYTES,
    )


def _rmsnorm_kernel(x_ref, g_ref, o_ref):
    x = x_ref[...].astype(F32)
    ms = jnp.mean(x * x, axis=-1, keepdims=True)
    o_ref[...] = (x * lax.rsqrt(ms + EPS) * g_ref[...]).astype(o_ref.dtype)


def rmsnorm(x, g, out_dtype, *, tm=256):
    m, d = x.shape
    return pl.pallas_call(
        _rmsnorm_kernel,
        grid=(m // tm,),
        in_specs=[pl.BlockSpec((tm, d), lambda i: (i, 0)),
                  pl.BlockSpec((1, d), lambda i: (0, 0))],
        out_specs=pl.BlockSpec((tm, d), lambda i: (i, 0)),
        out_shape=jax.ShapeDtypeStruct((m, d), out_dtype),
        compiler_params=_params(1),
        name="rmsnorm",
    )(x, g.reshape(1, d).astype(F32))


def _sigmoid(x):
    return 1.0 / (1.0 + jnp.exp(-x))


def _mm_kernel(*refs, nk, tm, epilogue, n_extra, b_is_nk):
    a_ref, bchunk_ref = refs[0], refs[1]
    extra = refs[2:2 + n_extra]
    o_ref = refs[2 + n_extra]
    bcast_ref = refs[3 + n_extra]
    acc_ref = refs[4 + n_extra]
    t = pl.program_id(0)
    i = pl.program_id(1)
    chunk_rows = bchunk_ref.shape[0]

    def stage():
        rows = pl.ds(pl.multiple_of(i * chunk_rows, chunk_rows), chunk_rows)
        bcast_ref[t % 2, rows, :] = bchunk_ref[...].astype(BF16)

    @pl.when(t == 0)
    def _():
        stage()

    @pl.when(t > 0)
    def _():
        stage()
        _mm_compute(a_ref, bcast_ref, extra, o_ref, acc_ref, t - 1, i, nk=nk, tm=tm,
                    epilogue=epilogue, b_is_nk=b_is_nk)


def _mm_compute(a_ref, bcast_ref, extra, o_ref, acc_ref, tile, i, *, nk, tm, epilogue, b_is_nk):
    k = tile % nk
    slot = tile % 2
    tn = o_ref.shape[1]
    strip = min(MXU_WIDTH, tn)
    rows = pl.ds(pl.multiple_of(i * tm, tm), tm)
    conv = epilogue.startswith("conv_silu")
    raw_rows = slice(SUBLANES, SUBLANES + tm) if conv else slice(0, tm)
    if conv:

        @pl.when(i == 0)
        def _():
            acc_ref[0:SUBLANES, :] = jnp.zeros((SUBLANES, tn), F32)

    for c in range(tn // strip):
        cols = slice(c * strip, (c + 1) * strip)
        if b_is_nk:
            acc = lax.dot_general(a_ref[...], bcast_ref[slot, cols, :], (((1,), (1,)), ((), ())),
                                  preferred_element_type=F32)
        else:
            acc = jnp.dot(a_ref[...], bcast_ref[slot, :, cols], preferred_element_type=F32)
        if nk > 1:
            acc_ref[rows, cols] = acc + jnp.where(k == 0, 0.0, acc_ref[rows, cols])
        else:
            acc_ref[raw_rows, cols] = acc
    for c in range(tn // strip):
        cols = slice(c * strip, (c + 1) * strip)
        acc = acc_ref[rows, cols] if nk > 1 else acc_ref[raw_rows, cols]

        if epilogue == "plain":
            val = acc
        elif epilogue == "relu2":
            r = jnp.maximum(acc, 0.0)
            val = r * r
        elif epilogue == "sigmoid":
            val = _sigmoid(acc)
        elif epilogue == "residual":
            val = extra[0][:, cols].astype(F32) + acc
        elif epilogue == "mul":
            val = extra[0][:, cols].astype(F32) * acc
        elif epilogue == "mul_add":
            val = extra[1][:, cols].astype(F32) + extra[0][:, cols].astype(F32) * acc
        elif epilogue == "ple":
            val = extra[0][:, cols].astype(F32) + _sigmoid(acc) * extra[1][:, cols].astype(F32)
        elif epilogue.startswith("conv_silu"):
            halo = SUBLANES
            cw = extra[0][:, cols]
            base = halo - (CONV_K - 1)
            y = acc * cw[CONV_K - 1:CONV_K]
            for tap in range(CONV_K - 1):
                y = y + acc_ref[base + tap:base + tap + tm, cols] * cw[tap:tap + 1]
            acc_ref[0:halo, cols] = acc[tm - halo:, :]
            y = y * _sigmoid(y)
            if epilogue == "conv_silu_l2norm":
                segs = []
                for hh in range(strip // HEAD_DIM):
                    seg = y[:, hh * HEAD_DIM:(hh + 1) * HEAD_DIM]
                    segs.append(seg * lax.rsqrt(jnp.sum(seg * seg, axis=-1, keepdims=True) + EPS))
                y = jnp.concatenate(segs, axis=1)
            val = y
        else:
            raise ValueError(epilogue)
        o_ref[:, cols] = val.astype(o_ref.dtype)


def matmul_ws(a, b, *, col0=0, n=None, out_dtype=BF16, epilogue="plain", extra=(),
              tm=1024, tn=512, tk=None, b_is_nk=False, name="matmul_ws"):
    m, kdim = a.shape
    n_total = b.shape[0] if b_is_nk else b.shape[1]
    if n is None:
        n = n_total - col0
    if tk is None:
        tk = kdim
    assert m % tm == 0 and n % tn == 0 and kdim % tk == 0 and col0 + n <= n_total
    nj, nk, ni = n // tn, kdim // tk, m // tm
    n_tiles = nj * nk

    def tile_jk(t):
        tile = jnp.maximum(t - 1, 0)
        return tile // nk, tile % nk

    def row_block(t, i):
        _, k = tile_jk(t)
        return jnp.where(jnp.logical_and(t > 0, k == nk - 1), i, 0)

    def a_map(t, i):
        return (jnp.where(t > 0, i, 0), tile_jk(t)[1])

    def out_map(t, i):
        return (row_block(t, i), tile_jk(t)[0])

    def staged(t, i):
        tile = jnp.minimum(t, n_tiles - 1)
        return tile // nk, tile % nk, jnp.where(t == n_tiles, ni - 1, i)

    if b_is_nk:
        assert col0 % SUBLANES == 0 and tn % (ni * SUBLANES) == 0
        cn = tn // ni

        def b_map(t, i):
            j, k, c = staged(t, i)
            return (pl.multiple_of(col0 + j * tn + c * cn, SUBLANES), pl.multiple_of(k * tk, LANES))

        b_spec = pl.BlockSpec((pl.Element(cn), pl.Element(tk)), b_map)
        bcast_shape = (2, tn, tk)
    else:
        assert col0 % tn == 0 and tk % (ni * SUBLANES) == 0

        def b_map(t, i):
            j, k, c = staged(t, i)
            return (k * ni + c, j + col0 // tn)

        b_spec = pl.BlockSpec((tk // ni, tn), b_map)
        bcast_shape = (2, tk, tn)
    in_specs = [pl.BlockSpec((tm, tk), a_map), b_spec]
    conv = epilogue.startswith("conv_silu")
    for idx, (_, ecol0) in enumerate(extra):
        assert ecol0 % tn == 0
        if conv and idx == 0:
            in_specs.append(pl.BlockSpec(
                (CONV_K, tn),
                functools.partial(lambda t, i, e0: (0, tile_jk(t)[0] + e0), e0=ecol0 // tn)))
        else:
            in_specs.append(pl.BlockSpec(
                (tm, tn),
                functools.partial(lambda t, i, e0: (row_block(t, i), tile_jk(t)[0] + e0),
                                  e0=ecol0 // tn)))
    scratch = [pltpu.VMEM(bcast_shape, BF16)]
    if conv:
        assert nk == 1
        scratch.append(pltpu.VMEM((SUBLANES + tm, tn), F32))
    else:
        scratch.append(pltpu.VMEM((m if nk > 1 else tm, tn), F32))
    return pl.pallas_call(
        functools.partial(_mm_kernel, nk=nk, tm=tm, epilogue=epilogue, n_extra=len(extra),
                          b_is_nk=b_is_nk),
        grid=(n_tiles + 1, ni),
        in_specs=in_specs,
        out_specs=pl.BlockSpec((tm, tn), out_map),
        out_shape=jax.ShapeDtypeStruct((m, n), out_dtype),
        scratch_shapes=scratch,
        compiler_params=_params(2),
        name=name,
    )(a, b, *[arr for arr, _ in extra])


def _attn_kernel(q_ref, kp_ref, kc_ref, vp_ref, vc_ref, o_ref, lse_ref, *, heads, scale):
    n = pl.program_id(1)
    span = q_ref.shape[0]
    qi = lax.broadcasted_iota(jnp.int32, (span, 2 * span), 0)
    kj = lax.broadcasted_iota(jnp.int32, (span, 2 * span), 1)
    dist = qi + span - kj
    valid = (dist >= 0) & (dist <= span) & ((kj >= span) | (n > 0))
    lane = lax.broadcasted_iota(jnp.int32, (span, HEAD_DIM), 1)
    lse_tile = jnp.zeros((span, HEAD_DIM), F32)
    for h in range(heads):
        cols = slice(h * HEAD_DIM, (h + 1) * HEAD_DIM)
        q = q_ref[:, cols]
        k = jnp.concatenate([kp_ref[:, cols], kc_ref[:, cols]], axis=0)
        v = jnp.concatenate([vp_ref[:, cols], vc_ref[:, cols]], axis=0)
        s = lax.dot_general(q, k, (((1,), (1,)), ((), ())), preferred_element_type=F32) * scale
        s = jnp.where(valid, s, -jnp.inf)
        m = jnp.max(s, axis=-1, keepdims=True)
        e = jnp.exp(s - m)
        l = jnp.sum(e, axis=-1, keepdims=True)
        o = jnp.dot(e.astype(BF16), v, preferred_element_type=F32) / l
        o_ref[:, cols] = o.astype(o_ref.dtype)
        lse_tile = jnp.where(lane == h, m + jnp.log(l), lse_tile)
    lse_ref[...] = lse_tile


def banded_attention(q, k, v, *, heads=ATTN_HEADS, col_blocks=(0, 0, 0)):
    d, length, _ = q.shape
    width = heads * HEAD_DIM
    span = ATTN_SPAN
    nb = length // span

    def cur_spec(cb):
        return pl.BlockSpec((None, span, width), lambda r, n: (r, n, cb))

    def prev_spec(cb):
        return pl.BlockSpec((None, span, width), lambda r, n: (r, jnp.maximum(n - 1, 0), cb))

    qc, kc, vc = col_blocks
    return pl.pallas_call(
        functools.partial(_attn_kernel, heads=heads, scale=HEAD_DIM ** -0.5),
        grid=(d, nb),
        in_specs=[cur_spec(qc), prev_spec(kc), cur_spec(kc), prev_spec(vc), cur_spec(vc)],
        out_specs=[pl.BlockSpec((None, span, width), lambda r, n: (r, n, 0)),
                   pl.BlockSpec((None, span, HEAD_DIM), lambda r, n: (r, n, 0))],
        out_shape=[jax.ShapeDtypeStruct((d, length, width), BF16),
                   jax.ShapeDtypeStruct((d, length, HEAD_DIM), F32)],
        compiler_params=_params(2),
        name="banded_attention",
    )(q, k, k, v, v)


def _attn_combine_kernel(o0_ref, o1_ref, o2_ref, l0_ref, l1_ref, l2_ref, out_ref, *, heads):
    l0, l1, l2 = l0_ref[...], l1_ref[...], l2_ref[...]
    m = jnp.maximum(jnp.maximum(l0, l1), l2)
    e0, e1, e2 = jnp.exp(l0 - m), jnp.exp(l1 - m), jnp.exp(l2 - m)
    den = e0 + e1 + e2
    w0, w1, w2 = e0 / den, e1 / den, e2 / den
    for h in range(heads):
        cols = slice(h * HEAD_DIM, (h + 1) * HEAD_DIM)
        acc = (w0[:, h:h + 1] * o0_ref[:, cols] + w1[:, h:h + 1] * o1_ref[:, cols]
               + w2[:, h:h + 1] * o2_ref[:, cols])
        out_ref[:, cols] = acc.astype(out_ref.dtype)


def attn_combine(outs, lses, *, tm=256):
    m, width = outs[0].shape
    heads = width // HEAD_DIM
    ospec = pl.BlockSpec((tm, width), lambda i: (i, 0))
    lspec = pl.BlockSpec((tm, HEAD_DIM), lambda i: (i, 0))
    return pl.pallas_call(
        functools.partial(_attn_combine_kernel, heads=heads),
        grid=(m // tm,),
        in_specs=[ospec] * 3 + [lspec] * 3,
        out_specs=ospec,
        out_shape=jax.ShapeDtypeStruct((m, width), BF16),
        compiler_params=_params(1),
        name="attn_combine",
    )(*outs, *lses)


def _softplus(x):
    return jnp.maximum(x, 0.0) + jnp.log(1.0 + jnp.exp(-jnp.abs(x)))


def _bdot(a, b, dims):
    return lax.dot_general(a, b, (dims, ((0,), (0,))), preferred_element_type=F32)


def _bmm(a, b):
    return _bdot(a, b, ((2,), (1,)))


def _split_bf16(x):
    hi = x.astype(BF16)
    lo = (x - hi.astype(F32)).astype(BF16)
    return hi, lo


def _bmm_split(a, b):
    a_hi, a_lo = _split_bf16(a)
    b_hi, b_lo = _split_bf16(b)
    return _bmm(a_hi, b_hi) + _bmm(a_hi, b_lo) + _bmm(a_lo, b_hi)


def _unit_lower_inverse_batched(a):
    c = a.shape[-1]
    eye = (lax.broadcasted_iota(jnp.int32, (c, c), 0)
           == lax.broadcasted_iota(jnp.int32, (c, c), 1)).astype(F32)
    p = eye - a
    x = a
    for _ in range(int(math.log2(c)) - 1):
        x = _bmm_split(x, x)
        p = p + _bmm_split(p, x)
    return p


def _dn_prep_kernel(q_ref, k_ref, v_ref, ba_ref, bat_ref, prow_ref, pcol_ref,
                    u_ref, w_ref, qg_ref, kd_ref, intra_ref, eg_ref, *, heads, chunks):
    c_len = DN_CHUNK

    ba = ba_ref[...]
    beta_col = _sigmoid(ba)
    g_col = -jnp.exp(prow_ref[0:1, :]) * _softplus(ba + prow_ref[1:2, :])
    row_in_chunk = lax.broadcasted_iota(jnp.int32, g_col.shape, 0) % c_len
    shift = 1
    while shift < c_len:
        g_col = g_col + jnp.where(row_in_chunk >= shift, pltpu.roll(g_col, shift, 0), 0.0)
        shift *= 2
    bat = bat_ref[...]
    beta_row = _sigmoid(bat)
    g_row = -jnp.exp(pcol_ref[:, 0:1])[None] * _softplus(bat + pcol_ref[:, 1:2][None])
    lane_in_chunk = lax.broadcasted_iota(jnp.int32, g_row.shape, 2)
    shift = 1
    while shift < c_len:
        g_row = g_row + jnp.where(lane_in_chunk >= shift, pltpu.roll(g_row, shift, 2), 0.0)
        shift *= 2
    eg_ref[...] = jnp.exp(g_row)

    ri = lax.broadcasted_iota(jnp.int32, (c_len, c_len), 0)
    ci = lax.broadcasted_iota(jnp.int32, (c_len, c_len), 1)
    lower = ri >= ci
    strict = ri > ci

    for c in range(chunks):
        rs = slice(c * c_len, (c + 1) * c_len)

        def per_head(x_ref):
            return jnp.stack([x_ref[rs, h * HEAD_DIM:(h + 1) * HEAD_DIM] for h in range(heads)])

        q = per_head(q_ref).astype(F32) * HEAD_DIM ** -0.5
        k16 = per_head(k_ref)
        k = k16.astype(F32)
        v16 = per_head(v_ref)
        gcol = jnp.stack([g_col[rs, heads + h:heads + h + 1] for h in range(heads)])
        bcol = jnp.stack([beta_col[rs, h:h + 1] for h in range(heads)])
        grow = jnp.stack([g_row[c, heads + h:heads + h + 1, :] for h in range(heads)])
        brow = jnp.stack([beta_row[c, h:h + 1, :] for h in range(heads)])
        decay = jnp.exp(jnp.where(lower, gcol - grow, -jnp.inf))
        q16 = q.astype(BF16)
        qk = _bdot(jnp.concatenate([q16, k16], axis=1), k16, ((2,), (2,)))
        intra = qk[:, :c_len] * decay
        a = jnp.where(strict, qk[:, c_len:] * decay * bcol, 0.0)
        t_inv = _unit_lower_inverse_batched(a)
        u = _bmm((t_inv * brow).astype(BF16), v16)
        w = _bmm((t_inv * (brow * jnp.exp(grow))).astype(BF16), k16)
        g_last = grow[:, :, c_len - 1:c_len]
        u_ref[:, rs, :] = u
        w_ref[:, rs, :] = w.astype(w_ref.dtype)
        qg_ref[:, rs, :] = (q * jnp.exp(gcol)).astype(qg_ref.dtype)
        kd_ref[:, rs, :] = (k * jnp.exp(g_last - gcol)).astype(kd_ref.dtype)
        intra_ref[:, rs, :] = intra.astype(intra_ref.dtype)


def _dn_scan_kernel(egl_ref, u_ref, w_ref, qg_ref, kd_ref, intra_ref, z_ref, gn_ref, o_ref,
                    state_ref, *, heads, chunks):
    tb = pl.program_id(0)
    c_len = DN_CHUNK

    @pl.when(tb == 0)
    def _():
        state_ref[...] = jnp.zeros_like(state_ref)

    gn = gn_ref[...]
    for c in range(chunks):
        rs = slice(c * c_len, (c + 1) * c_len)
        s = state_ref[...]
        lhs = jnp.concatenate([w_ref[:, rs, :], qg_ref[:, rs, :]], axis=1)
        r = _bmm(lhs, s.astype(BF16))
        v_new = u_ref[:, rs, :] - r[:, :c_len]
        vn16 = v_new.astype(BF16)
        o = r[:, c_len:] + _bmm(intra_ref[:, rs, :], vn16)
        upd = _bdot(kd_ref[:, rs, :], vn16, ((1,), (1,)))
        base = (tb * chunks + c) * heads
        for h in range(heads):
            state_ref[h] = s[h] * egl_ref[base + h] + upd[h]
        o = o * lax.rsqrt(jnp.mean(o * o, axis=-1, keepdims=True) + EPS) * gn
        for h in range(heads):
            cs = slice(h * HEAD_DIM, (h + 1) * HEAD_DIM)
            zz = z_ref[rs, cs].astype(F32)
            o_ref[rs, cs] = (o[h] * (zz * _sigmoid(zz))).astype(o_ref.dtype)


def gated_deltanet2(qk, v, ba, a_log, dt_bias, z, dn_norm, *, prep_chunks=2, scan_chunks=4):
    s_len, width = z.shape
    heads = width // HEAD_DIM
    c_len = DN_CHUNK
    n_chunks = s_len // c_len
    bat = ba[:, :2 * heads].reshape(n_chunks, c_len, 2 * heads).transpose(0, 2, 1)
    prow = jnp.zeros((2, HEAD_DIM), F32)
    prow = prow.at[0, heads:2 * heads].set(a_log.astype(F32))
    prow = prow.at[1, heads:2 * heads].set(dt_bias.astype(F32))
    pcol = jnp.zeros((2 * heads, 2), F32)
    pcol = pcol.at[heads:, 0].set(a_log.astype(F32)).at[heads:, 1].set(dt_bias.astype(F32))

    rows = prep_chunks * c_len

    def main_spec(i):
        return pl.BlockSpec((rows, width), lambda t: (t, i))

    def prep_out(last):
        return pl.BlockSpec((heads, rows, last), lambda t: (0, t, 0))

    u, w, qg, kd, intra, eg = pl.pallas_call(
        functools.partial(_dn_prep_kernel, heads=heads, chunks=prep_chunks),
        grid=(s_len // rows,),
        in_specs=[main_spec(0), main_spec(1), main_spec(0),
                  pl.BlockSpec((rows, HEAD_DIM), lambda t: (t, 0)),
                  pl.BlockSpec((prep_chunks, 2 * heads, c_len), lambda t: (t, 0, 0)),
                  pl.BlockSpec((2, HEAD_DIM), lambda t: (0, 0)),
                  pl.BlockSpec((2 * heads, 2), lambda t: (0, 0))],
        out_specs=[prep_out(HEAD_DIM), prep_out(HEAD_DIM), prep_out(HEAD_DIM),
                   prep_out(HEAD_DIM), prep_out(c_len),
                   pl.BlockSpec((prep_chunks, 2 * heads, c_len), lambda t: (t, 0, 0))],
        out_shape=[jax.ShapeDtypeStruct((heads, s_len, HEAD_DIM), F32),
                   jax.ShapeDtypeStruct((heads, s_len, HEAD_DIM), BF16),
                   jax.ShapeDtypeStruct((heads, s_len, HEAD_DIM), BF16),
                   jax.ShapeDtypeStruct((heads, s_len, HEAD_DIM), BF16),
                   jax.ShapeDtypeStruct((heads, s_len, c_len), BF16),
                   jax.ShapeDtypeStruct((n_chunks, 2 * heads, c_len), F32)],
        compiler_params=_params(1),
        name="deltanet_prep",
    )(qk, qk, v, ba, bat, prow, pcol)

    egl = eg[:, heads:, c_len - 1].reshape(n_chunks * heads)

    scan_rows = scan_chunks * c_len

    def head_major(last):
        return pl.BlockSpec((heads, scan_rows, last), lambda t: (0, t, 0))

    nat = pl.BlockSpec((scan_rows, width), lambda t: (t, 0))
    return pl.pallas_call(
        functools.partial(_dn_scan_kernel, heads=heads, chunks=scan_chunks),
        grid=(s_len // scan_rows,),
        in_specs=[pl.BlockSpec(memory_space=pltpu.SMEM),
                  head_major(HEAD_DIM), head_major(HEAD_DIM), head_major(HEAD_DIM),
                  head_major(HEAD_DIM), head_major(c_len), nat,
                  pl.BlockSpec((1, HEAD_DIM), lambda t: (0, 0))],
        out_specs=nat,
        out_shape=jax.ShapeDtypeStruct((s_len, width), BF16),
        scratch_shapes=[pltpu.VMEM((heads, HEAD_DIM, HEAD_DIM), F32)],
        compiler_params=_params(1),
        name="deltanet_scan",
    )(egl, u, w, qg, kd, intra, z, dn_norm.reshape(1, HEAD_DIM).astype(F32))


def _proj_norm_kernel(p_ref, w_ref, g_ref, o_ref, wcast_ref):
    @pl.when(pl.program_id(0) == 0)
    def _():
        wcast_ref[...] = w_ref[...].astype(BF16)

    y = jnp.dot(p_ref[...].astype(BF16), wcast_ref[...], preferred_element_type=F32)
    ms = jnp.mean(y * y, axis=-1, keepdims=True)
    o_ref[...] = (y * lax.rsqrt(ms + EPS) * g_ref[...]).astype(o_ref.dtype)


def proj_norm(p, w, g, out_dtype, *, tm=512):
    m, kdim = p.shape
    n = w.shape[1]
    return pl.pallas_call(
        _proj_norm_kernel,
        grid=(m // tm,),
        in_specs=[pl.BlockSpec((tm, kdim), lambda i: (i, 0)),
                  pl.BlockSpec((kdim, n), lambda i: (0, 0)),
                  pl.BlockSpec((1, n), lambda i: (0, 0))],
        out_specs=pl.BlockSpec((tm, n), lambda i: (i, 0)),
        out_shape=jax.ShapeDtypeStruct((m, n), out_dtype),
        scratch_shapes=[pltpu.VMEM((kdim, n), BF16)],
        compiler_params=_params(1),
        name="proj_norm",
    )(p, w, g.reshape(1, n).astype(F32))


def _layer(x, p, w_in, conv_w, a_log, dt_bias, dn_norm, w_attn_up, w_dn_up, w_out, w_mlp_up,
           w_mlp_down, w_ple_gate, w_ple_proj, norm_mix, norm_mlp, norm_ple, ple_post_norm):
    s_len, d_model = x.shape
    attn_w = ATTN_HEADS * HEAD_DIM
    dn_w = DN_HEADS * HEAD_DIM
    n_groups = len(ATTN_DILATIONS)
    c_qkv_b = 3 * n_groups * attn_w
    c_z = c_qkv_b + 3 * dn_w
    c_ba = c_z + dn_w
    c_gate = c_ba + 2 * DN_HEADS
    mm = matmul_ws

    w_in_t = w_in.T
    h = rmsnorm(x, norm_mix, BF16)
    qkv_a = mm(h, w_in_t, b_is_nk=True, col0=0, n=c_qkv_b, tn=1024, out_dtype=BF16,
               name="proj_attn")
    qk_b = mm(h, w_in_t, b_is_nk=True, col0=c_qkv_b, n=2 * dn_w, tn=1024, out_dtype=BF16,
              epilogue="conv_silu_l2norm", extra=((conv_w, 0),), name="proj_dn_qk")
    v_b = mm(h, w_in_t, b_is_nk=True, col0=c_qkv_b + 2 * dn_w, n=dn_w, tn=1024, out_dtype=BF16,
             epilogue="conv_silu", extra=((conv_w, 2 * dn_w),), name="proj_dn_v")
    z_b = mm(h, w_in_t, b_is_nk=True, col0=c_z, n=dn_w, tn=1024, out_dtype=BF16, name="proj_z")
    ba = mm(h, w_in_t, b_is_nk=True, col0=c_ba, n=LANES, tn=LANES, out_dtype=F32, name="proj_ba")
    gates = mm(h, w_in_t, b_is_nk=True, col0=c_gate, n=2 * d_model, tn=1024, out_dtype=BF16,
               epilogue="sigmoid", name="proj_gates")

    outs, lses = [], []
    for gi, dil in enumerate(ATTN_DILATIONS):
        if dil == 1:
            arr = qkv_a.reshape(1, s_len, c_qkv_b)
            o_g, lse_g = banded_attention(
                arr, arr, arr, col_blocks=(gi, n_groups + gi, 2 * n_groups + gi))
        else:
            def sub(t):
                cols = qkv_a[:, (t * n_groups + gi) * attn_w:(t * n_groups + gi + 1) * attn_w]
                return cols.reshape(s_len // dil, dil, attn_w).transpose(1, 0, 2)
            o_g, lse_g = banded_attention(sub(0), sub(1), sub(2))
        outs.append(o_g.transpose(1, 0, 2).reshape(s_len, attn_w))
        lses.append(lse_g.transpose(1, 0, 2).reshape(s_len, HEAD_DIM))
    o_a = attn_combine(outs, lses)

    o_b = gated_deltanet2(qk_b, v_b, ba, a_log, dt_bias, z_b, dn_norm)

    t_a = mm(o_a, w_attn_up, tn=1024, out_dtype=BF16, epilogue="mul", extra=((gates, 0),),
             name="attn_up")
    merged = mm(o_b, w_dn_up, tn=1024, out_dtype=BF16, epilogue="mul_add",
                extra=((gates, d_model), (t_a, 0)), name="dn_up")
    x = mm(merged, w_out, tm=512, tn=1024, out_dtype=F32, epilogue="residual", extra=((x, 0),),
           name="out_proj")

    h = rmsnorm(x, norm_mlp, BF16)
    u = mm(h, w_mlp_up, tn=1024, out_dtype=BF16, epilogue="relu2", name="mlp_up")
    x = mm(u, w_mlp_down, out_dtype=F32, epilogue="residual", extra=((x, 0),), tk=4096,
           name="mlp_down")

    h = rmsnorm(x, norm_ple, BF16)
    pp = proj_norm(p, w_ple_proj, ple_post_norm, F32)
    x = mm(h, w_ple_gate, tm=512, tn=1024, out_dtype=F32, epilogue="ple",
           extra=((x, 0), (pp, 0)), name="ple_gate")
    return x


def kernel(x, p, w_in, conv_w, dn_a_log, dn_dt_bias, dn_norm, w_attn_up, w_dn_up, w_out,
           w_mlp_up, w_mlp_down, w_ple_gate, w_ple_proj, norm_mix, norm_mlp, norm_ple,
           ple_post_norm, final_norm):
    b, s_len, d_model = x.shape
    assert b == 1
    depth = w_in.shape[0]
    xs = x.reshape(s_len, d_model)
    for i in range(depth):
        xs = _layer(xs, p[i, 0], w_in[i], conv_w[i], dn_a_log[i], dn_dt_bias[i], dn_norm[i],
                    w_attn_up[i], w_dn_up[i], w_out[i], w_mlp_up[i], w_mlp_down[i],
                    w_ple_gate[i], w_ple_proj[i], norm_mix[i], norm_mlp[i], norm_ple[i],
                    ple_post_norm[i])
    out = rmsnorm(xs, final_norm, x.dtype)
    return out.reshape(b, s_len, d_model)
```

```python
import functools
import math

import jax
import jax.numpy as jnp
from jax import lax
from jax.experimental import pallas as pl
from jax.experimental.pallas import tpu as pltpu

F32 = jnp.float32
BF16 = jnp.bfloat16

EPS = 1e-6
LANES = 128
SUBLANES = 8
MXU_WIDTH = 256
HEAD_DIM = 128
ATTN_SPAN = 128
ATTN_DILATIONS = (1, 4, 16)
ATTN_HEADS = 8
DN_HEADS = 16
DN_CHUNK = 64
CONV_K = 4

VMEM_LIMIT_BYTES = 56 * 1024 * 1024


def _params(n_axes):
    return pltpu.CompilerParams(
        dimension_semantics=("arbitrary",) * n_axes,
        vmem_limit_bytes=VMEM_LIMIT_BYTES,
    )


def _rmsnorm_kernel(x_ref, g_ref, o_ref):
    x = x_ref[...].astype(F32)
    ms = jnp.mean(x * x, axis=-1, keepdims=True)
    o_ref[...] = (x * lax.rsqrt(ms + EPS) * g_ref[...]).astype(o_ref.dtype)


def rmsnorm(x, g, out_dtype, *, tm=256):
    m, d = x.shape
    return pl.pallas_call(
        _rmsnorm_kernel,
        grid=(m // tm,),
        in_specs=[pl.BlockSpec((tm, d), lambda i: (i, 0)),
                  pl.BlockSpec((1, d), lambda i: (0, 0))],
        out_specs=pl.BlockSpec((tm, d), lambda i: (i, 0)),
        out_shape=jax.ShapeDtypeStruct((m, d), out_dtype),
        compiler_params=_params(1),
        name="rmsnorm",
    )(x, g.reshape(1, d).astype(F32))


def _sigmoid(x):
    return 1.0 / (1.0 + jnp.exp(-x))


def _mm_kernel(*refs, nk, tm, epilogue, n_extra, b_is_nk):
    a_ref, bchunk_ref = refs[0], refs[1]
    extra = refs[2:2 + n_extra]
    o_ref = refs[2 + n_extra]
    bcast_ref = refs[3 + n_extra]
    acc_ref = refs[4 + n_extra]
    t = pl.program_id(0)
    i = pl.program_id(1)
    chunk_rows = bchunk_ref.shape[0]

    def stage():
        rows = pl.ds(pl.multiple_of(i * chunk_rows, chunk_rows), chunk_rows)
        bcast_ref[t % 2, rows, :] = bchunk_ref[...].astype(BF16)

    @pl.when(t == 0)
    def _():
        stage()

    @pl.when(t > 0)
    def _():
        stage()
        _mm_compute(a_ref, bcast_ref, extra, o_ref, acc_ref, t - 1, i, nk=nk, tm=tm,
                    epilogue=epilogue, b_is_nk=b_is_nk)


def _mm_compute(a_ref, bcast_ref, extra, o_ref, acc_ref, tile, i, *, nk, tm, epilogue, b_is_nk):
    k = tile % nk
    slot = tile % 2
    tn = o_ref.shape[1]
    strip = min(MXU_WIDTH, tn)
    rows = pl.ds(pl.multiple_of(i * tm, tm), tm)
    conv = epilogue.startswith("conv_silu")
    raw_rows = slice(SUBLANES, SUBLANES + tm) if conv else slice(0, tm)
    if conv:

        @pl.when(i == 0)
        def _():
            acc_ref[0:SUBLANES, :] = jnp.zeros((SUBLANES, tn), F32)

    for c in range(tn // strip):
        cols = slice(c * strip, (c + 1) * strip)
        if b_is_nk:
            acc = lax.dot_general(a_ref[...], bcast_ref[slot, cols, :], (((1,), (1,)), ((), ())),
                                  preferred_element_type=F32)
        else:
            acc = jnp.dot(a_ref[...], bcast_ref[slot, :, cols], preferred_element_type=F32)
        if nk > 1:
            acc_ref[rows, cols] = acc + jnp.where(k == 0, 0.0, acc_ref[rows, cols])
        else:
            acc_ref[raw_rows, cols] = acc
    for c in range(tn // strip):
        cols = slice(c * strip, (c + 1) * strip)
        acc = acc_ref[rows, cols] if nk > 1 else acc_ref[raw_rows, cols]

        if epilogue == "plain":
            val = acc
        elif epilogue == "relu2":
            r = jnp.maximum(acc, 0.0)
            val = r * r
        elif epilogue == "sigmoid":
            val = _sigmoid(acc)
        elif epilogue == "residual":
            val = extra[0][:, cols].astype(F32) + acc
        elif epilogue == "mul":
            val = extra[0][:, cols].astype(F32) * acc
        elif epilogue == "mul_add":
            val = extra[1][:, cols].astype(F32) + extra[0][:, cols].astype(F32) * acc
        elif epilogue == "ple":
            val = extra[0][:, cols].astype(F32) + _sigmoid(acc) * extra[1][:, cols].astype(F32)
        elif epilogue.startswith("conv_silu"):
            halo = SUBLANES
            cw = extra[0][:, cols]
            base = halo - (CONV_K - 1)
            y = acc * cw[CONV_K - 1:CONV_K]
            for tap in range(CONV_K - 1):
                y = y + acc_ref[base + tap:base + tap + tm, cols] * cw[tap:tap + 1]
            acc_ref[0:halo, cols] = acc[tm - halo:, :]
            y = y * _sigmoid(y)
            if epilogue == "conv_silu_l2norm":
                segs = []
                for hh in range(strip // HEAD_DIM):
                    seg = y[:, hh * HEAD_DIM:(hh + 1) * HEAD_DIM]
                    segs.append(seg * lax.rsqrt(jnp.sum(seg * seg, axis=-1, keepdims=True) + EPS))
                y = jnp.concatenate(segs, axis=1)
            val = y
        else:
            raise ValueError(epilogue)
        o_ref[:, cols] = val.astype(o_ref.dtype)


def matmul_ws(a, b, *, col0=0, n=None, out_dtype=BF16, epilogue="plain", extra=(),
              tm=1024, tn=512, tk=None, b_is_nk=False, name="matmul_ws"):
    m, kdim = a.shape
    n_total = b.shape[0] if b_is_nk else b.shape[1]
    if n is None:
        n = n_total - col0
    if tk is None:
        tk = kdim
    assert m % tm == 0 and n % tn == 0 and kdim % tk == 0 and col0 + n <= n_total
    nj, nk, ni = n // tn, kdim // tk, m // tm
    n_tiles = nj * nk

    def tile_jk(t):
        tile = jnp.maximum(t - 1, 0)
        return tile // nk, tile % nk

    def row_block(t, i):
        _, k = tile_jk(t)
        return jnp.where(jnp.logical_and(t > 0, k == nk - 1), i, 0)

    def a_map(t, i):
        return (jnp.where(t > 0, i, 0), tile_jk(t)[1])

    def out_map(t, i):
        return (row_block(t, i), tile_jk(t)[0])

    def staged(t, i):
        tile = jnp.minimum(t, n_tiles - 1)
        return tile // nk, tile % nk, jnp.where(t == n_tiles, ni - 1, i)

    if b_is_nk:
        assert col0 % SUBLANES == 0 and tn % (ni * SUBLANES) == 0
        cn = tn // ni

        def b_map(t, i):
            j, k, c = staged(t, i)
            return (pl.multiple_of(col0 + j * tn + c * cn, SUBLANES), pl.multiple_of(k * tk, LANES))

        b_spec = pl.BlockSpec((pl.Element(cn), pl.Element(tk)), b_map)
        bcast_shape = (2, tn, tk)
    else:
        assert col0 % tn == 0 and tk % (ni * SUBLANES) == 0

        def b_map(t, i):
            j, k, c = staged(t, i)
            return (k * ni + c, j + col0 // tn)

        b_spec = pl.BlockSpec((tk // ni, tn), b_map)
        bcast_shape = (2, tk, tn)
    in_specs = [pl.BlockSpec((tm, tk), a_map), b_spec]
    conv = epilogue.startswith("conv_silu")
    for idx, (_, ecol0) in enumerate(extra):
        assert ecol0 % tn == 0
        if conv and idx == 0:
            in_specs.append(pl.BlockSpec(
                (CONV_K, tn),
                functools.partial(lambda t, i, e0: (0, tile_jk(t)[0] + e0), e0=ecol0 // tn)))
        else:
            in_specs.append(pl.BlockSpec(
                (tm, tn),
                functools.partial(lambda t, i, e0: (row_block(t, i), tile_jk(t)[0] + e0),
                                  e0=ecol0 // tn)))
    scratch = [pltpu.VMEM(bcast_shape, BF16)]
    if conv:
        assert nk == 1
        scratch.append(pltpu.VMEM((SUBLANES + tm, tn), F32))
    else:
        scratch.append(pltpu.VMEM((m if nk > 1 else tm, tn), F32))
    return pl.pallas_call(
        functools.partial(_mm_kernel, nk=nk, tm=tm, epilogue=epilogue, n_extra=len(extra),
                          b_is_nk=b_is_nk),
        grid=(n_tiles + 1, ni),
        in_specs=in_specs,
        out_specs=pl.BlockSpec((tm, tn), out_map),
        out_shape=jax.ShapeDtypeStruct((m, n), out_dtype),
        scratch_shapes=scratch,
        compiler_params=_params(2),
        name=name,
    )(a, b, *[arr for arr, _ in extra])


def _attn_kernel(q_ref, kp_ref, kc_ref, vp_ref, vc_ref, o_ref, lse_ref, *, heads, scale):
    n = pl.program_id(1)
    planes, sub, _ = q_ref.shape
    span = planes * sub
    rho = lax.broadcasted_iota(jnp.int32, (span, 2 * span), 0)
    kap = lax.broadcasted_iota(jnp.int32, (span, 2 * span), 1)
    in_cur = kap // span
    dist = (planes * (sub * (1 - in_cur) + rho % sub - kap % sub)
            + rho // sub - (kap % span) // sub)
    valid = (dist >= 0) & (dist <= span) & ((in_cur == 1) | (n > 0))
    lane = lax.broadcasted_iota(jnp.int32, (span, HEAD_DIM), 1)
    lse_tile = jnp.zeros((span, HEAD_DIM), F32)

    def tile(ref, cols):
        return jnp.concatenate([ref[a, :, cols] for a in range(planes)], axis=0)

    for h in range(heads):
        cols = slice(h * HEAD_DIM, (h + 1) * HEAD_DIM)
        q = tile(q_ref, cols)
        k = jnp.concatenate([tile(kp_ref, cols), tile(kc_ref, cols)], axis=0)
        v = jnp.concatenate([tile(vp_ref, cols), tile(vc_ref, cols)], axis=0)
        s = lax.dot_general(q, k, (((1,), (1,)), ((), ())), preferred_element_type=F32) * scale
        s = jnp.where(valid, s, -jnp.inf)
        m = jnp.max(s, axis=-1, keepdims=True)
        e = jnp.exp(s - m)
        l = jnp.sum(e, axis=-1, keepdims=True)
        o = (jnp.dot(e.astype(BF16), v, preferred_element_type=F32) / l).astype(o_ref.dtype)
        for a in range(planes):
            o_ref[a, :, cols] = o[a * sub:(a + 1) * sub]
        lse_tile = jnp.where(lane == h, m + jnp.log(l), lse_tile)
    for a in range(planes):
        lse_ref[a] = lse_tile[a * sub:(a + 1) * sub]


def banded_attention(q, k, v, *, heads=ATTN_HEADS, col_blocks=(0, 0, 0)):
    planes, n_sub, rows, _ = q.shape
    width = heads * HEAD_DIM
    sub = ATTN_SPAN // planes
    nb = rows // sub

    def cur_spec(cb):
        return pl.BlockSpec((planes, None, sub, width), lambda r, n: (0, r, n, cb))

    def prev_spec(cb):
        return pl.BlockSpec((planes, None, sub, width),
                            lambda r, n: (0, r, jnp.maximum(n - 1, 0), cb))

    qc, kc, vc = col_blocks
    return pl.pallas_call(
        functools.partial(_attn_kernel, heads=heads, scale=HEAD_DIM ** -0.5),
        grid=(n_sub, nb),
        in_specs=[cur_spec(qc), prev_spec(kc), cur_spec(kc), prev_spec(vc), cur_spec(vc)],
        out_specs=[pl.BlockSpec((planes, None, sub, width), lambda r, n: (0, r, n, 0)),
                   pl.BlockSpec((planes, None, sub, HEAD_DIM), lambda r, n: (0, r, n, 0))],
        out_shape=[jax.ShapeDtypeStruct((planes, n_sub, rows, width), BF16),
                   jax.ShapeDtypeStruct((planes, n_sub, rows, HEAD_DIM), F32)],
        compiler_params=_params(2),
        name="banded_attention",
    )(q, k, k, v, v)


def _attn_combine_kernel(o0_ref, o1_ref, o2_ref, l0_ref, l1_ref, l2_ref, out_ref, *, heads):
    l0, l1, l2 = l0_ref[...], l1_ref[...], l2_ref[...]
    m = jnp.maximum(jnp.maximum(l0, l1), l2)
    e0, e1, e2 = jnp.exp(l0 - m), jnp.exp(l1 - m), jnp.exp(l2 - m)
    den = e0 + e1 + e2
    w0, w1, w2 = e0 / den, e1 / den, e2 / den
    for h in range(heads):
        cols = slice(h * HEAD_DIM, (h + 1) * HEAD_DIM)
        acc = (w0[:, h:h + 1] * o0_ref[:, cols] + w1[:, h:h + 1] * o1_ref[:, cols]
               + w2[:, h:h + 1] * o2_ref[:, cols])
        out_ref[:, cols] = acc.astype(out_ref.dtype)


def attn_combine(outs, lses, *, tm=256):
    m, width = outs[0].shape
    heads = width // HEAD_DIM
    ospec = pl.BlockSpec((tm, width), lambda i: (i, 0))
    lspec = pl.BlockSpec((tm, HEAD_DIM), lambda i: (i, 0))
    return pl.pallas_call(
        functools.partial(_attn_combine_kernel, heads=heads),
        grid=(m // tm,),
        in_specs=[ospec] * 3 + [lspec] * 3,
        out_specs=ospec,
        out_shape=jax.ShapeDtypeStruct((m, width), BF16),
        compiler_params=_params(1),
        name="attn_combine",
    )(*outs, *lses)


def _softplus(x):
    return jnp.maximum(x, 0.0) + jnp.log(1.0 + jnp.exp(-jnp.abs(x)))


def _bdot(a, b, dims):
    return lax.dot_general(a, b, (dims, ((0,), (0,))), preferred_element_type=F32)


def _bmm(a, b):
    return _bdot(a, b, ((2,), (1,)))


def _split_bf16(x):
    hi = x.astype(BF16)
    lo = (x - hi.astype(F32)).astype(BF16)
    return hi, lo


def _bmm_split(a, b):
    a_hi, a_lo = _split_bf16(a)
    b_hi, b_lo = _split_bf16(b)
    return _bmm(a_hi, b_hi) + _bmm(a_hi, b_lo) + _bmm(a_lo, b_hi)


def _unit_lower_inverse_batched(a):
    c = a.shape[-1]
    b0 = SUBLANES
    ri = lax.broadcasted_iota(jnp.int32, (c, c), 0)
    ci = lax.broadcasted_iota(jnp.int32, (c, c), 1)
    below = ri > ci
    eye = (ri == ci).astype(F32)

    def same_block(b):
        return (ri // b) == (ci // b)

    def mm(x, y):
        return _bmm(x.astype(BF16), y.astype(BF16))

    d = jnp.where(same_block(b0) & below, a, 0.0)
    p = eye - d
    x = d
    for _ in range(int(math.log2(b0)) - 1):
        x = mm(x, x)
        p = p + mm(p, x)
    b = b0
    while b < c:
        off_diag = jnp.where(same_block(2 * b) & jnp.logical_not(same_block(b)) & below, a, 0.0)
        p = p - mm(mm(p, off_diag), p)
        b *= 2
    resid = eye - p - _bmm_split(a, p)
    return p + mm(p, resid)


def _dn_prep_kernel(q_ref, k_ref, v_ref, ba_ref, bat_ref, prow_ref, pcol_ref,
                    u_ref, w_ref, qg_ref, kd_ref, intra_ref, eg_ref, *, heads, chunks):
    c_len = DN_CHUNK

    ba = ba_ref[...]
    beta_col = _sigmoid(ba)
    g_col = -jnp.exp(prow_ref[0:1, :]) * _softplus(ba + prow_ref[1:2, :])
    row_in_chunk = lax.broadcasted_iota(jnp.int32, g_col.shape, 0) % c_len
    shift = 1
    while shift < c_len:
        g_col = g_col + jnp.where(row_in_chunk >= shift, pltpu.roll(g_col, shift, 0), 0.0)
        shift *= 2
    bat = bat_ref[...]
    beta_row = _sigmoid(bat)
    g_row = -jnp.exp(pcol_ref[:, 0:1])[None] * _softplus(bat + pcol_ref[:, 1:2][None])
    lane_in_chunk = lax.broadcasted_iota(jnp.int32, g_row.shape, 2)
    shift = 1
    while shift < c_len:
        g_row = g_row + jnp.where(lane_in_chunk >= shift, pltpu.roll(g_row, shift, 2), 0.0)
        shift *= 2
    eg_ref[...] = jnp.exp(g_row)

    ri = lax.broadcasted_iota(jnp.int32, (c_len, c_len), 0)
    ci = lax.broadcasted_iota(jnp.int32, (c_len, c_len), 1)
    lower = ri >= ci
    strict = ri > ci

    units = [(c, h) for c in range(chunks) for h in range(heads)]

    def rows_of(c):
        return slice(c * c_len, (c + 1) * c_len)

    def per_unit(x_ref):
        return jnp.stack([x_ref[rows_of(c), h * HEAD_DIM:(h + 1) * HEAD_DIM] for c, h in units])

    q = per_unit(q_ref).astype(F32) * HEAD_DIM ** -0.5
    k16 = per_unit(k_ref)
    k = k16.astype(F32)
    v16 = per_unit(v_ref)
    gcol = jnp.stack([g_col[rows_of(c), heads + h:heads + h + 1] for c, h in units])
    bcol = jnp.stack([beta_col[rows_of(c), h:h + 1] for c, h in units])
    grow = jnp.stack([g_row[c, heads + h:heads + h + 1, :] for c, h in units])
    brow = jnp.stack([beta_row[c, h:h + 1, :] for c, h in units])
    decay = jnp.exp(jnp.where(lower, gcol - grow, -jnp.inf))
    q16 = q.astype(BF16)
    qk = _bdot(jnp.concatenate([q16, k16], axis=1), k16, ((2,), (2,)))
    intra = qk[:, :c_len] * decay
    a = jnp.where(strict, qk[:, c_len:] * decay * bcol, 0.0)
    t_inv = _unit_lower_inverse_batched(a)
    u = _bmm((t_inv * brow).astype(BF16), v16)
    w = _bmm((t_inv * (brow * jnp.exp(grow))).astype(BF16), k16)
    g_last = grow[:, :, c_len - 1:c_len]
    qg = q * jnp.exp(gcol)
    kd = k * jnp.exp(g_last - gcol)
    for c in range(chunks):
        rs, sl = rows_of(c), slice(c * heads, (c + 1) * heads)
        u_ref[:, rs, :] = u[sl]
        w_ref[:, rs, :] = w[sl].astype(w_ref.dtype)
        qg_ref[:, rs, :] = qg[sl].astype(qg_ref.dtype)
        kd_ref[:, rs, :] = kd[sl].astype(kd_ref.dtype)
        intra_ref[:, rs, :] = intra[sl].astype(intra_ref.dtype)


def _dn_scan_kernel(egl_ref, u_ref, w_ref, qg_ref, kd_ref, intra_ref, z_ref, gn_ref, o_ref,
                    state_ref, *, heads, chunks):
    tb = pl.program_id(0)
    c_len = DN_CHUNK

    @pl.when(tb == 0)
    def _():
        state_ref[...] = jnp.zeros_like(state_ref)

    gn = gn_ref[...]
    for c in range(chunks):
        rs = slice(c * c_len, (c + 1) * c_len)
        s = state_ref[...]
        lhs = jnp.concatenate([w_ref[:, rs, :], qg_ref[:, rs, :]], axis=1)
        r = _bmm(lhs, s.astype(BF16))
        v_new = u_ref[:, rs, :] - r[:, :c_len]
        vn16 = v_new.astype(BF16)
        o = r[:, c_len:] + _bmm(intra_ref[:, rs, :], vn16)
        upd = _bdot(kd_ref[:, rs, :], vn16, ((1,), (1,)))
        base = (tb * chunks + c) * heads
        for h in range(heads):
            state_ref[h] = s[h] * egl_ref[base + h] + upd[h]
        o = o * lax.rsqrt(jnp.mean(o * o, axis=-1, keepdims=True) + EPS) * gn
        for h in range(heads):
            cs = slice(h * HEAD_DIM, (h + 1) * HEAD_DIM)
            zz = z_ref[rs, cs].astype(F32)
            o_ref[rs, cs] = (o[h] * (zz * _sigmoid(zz))).astype(o_ref.dtype)


def gated_deltanet2(qk, v, ba, a_log, dt_bias, z, dn_norm, *, prep_chunks=4, scan_chunks=4):
    s_len, width = z.shape
    heads = width // HEAD_DIM
    c_len = DN_CHUNK
    n_chunks = s_len // c_len
    bat = ba[:, :2 * heads].reshape(n_chunks, c_len, 2 * heads).transpose(0, 2, 1)
    prow = jnp.zeros((2, HEAD_DIM), F32)
    prow = prow.at[0, heads:2 * heads].set(a_log.astype(F32))
    prow = prow.at[1, heads:2 * heads].set(dt_bias.astype(F32))
    pcol = jnp.zeros((2 * heads, 2), F32)
    pcol = pcol.at[heads:, 0].set(a_log.astype(F32)).at[heads:, 1].set(dt_bias.astype(F32))

    rows = prep_chunks * c_len

    def main_spec(i):
        return pl.BlockSpec((rows, width), lambda t: (t, i))

    def prep_out(last):
        return pl.BlockSpec((heads, rows, last), lambda t: (0, t, 0))

    u, w, qg, kd, intra, eg = pl.pallas_call(
        functools.partial(_dn_prep_kernel, heads=heads, chunks=prep_chunks),
        grid=(s_len // rows,),
        in_specs=[main_spec(0), main_spec(1), main_spec(0),
                  pl.BlockSpec((rows, HEAD_DIM), lambda t: (t, 0)),
                  pl.BlockSpec((prep_chunks, 2 * heads, c_len), lambda t: (t, 0, 0)),
                  pl.BlockSpec((2, HEAD_DIM), lambda t: (0, 0)),
                  pl.BlockSpec((2 * heads, 2), lambda t: (0, 0))],
        out_specs=[prep_out(HEAD_DIM), prep_out(HEAD_DIM), prep_out(HEAD_DIM),
                   prep_out(HEAD_DIM), prep_out(c_len),
                   pl.BlockSpec((prep_chunks, 2 * heads, c_len), lambda t: (t, 0, 0))],
        out_shape=[jax.ShapeDtypeStruct((heads, s_len, HEAD_DIM), F32),
                   jax.ShapeDtypeStruct((heads, s_len, HEAD_DIM), BF16),
                   jax.ShapeDtypeStruct((heads, s_len, HEAD_DIM), BF16),
                   jax.ShapeDtypeStruct((heads, s_len, HEAD_DIM), BF16),
                   jax.ShapeDtypeStruct((heads, s_len, c_len), BF16),
                   jax.ShapeDtypeStruct((n_chunks, 2 * heads, c_len), F32)],
        compiler_params=_params(1),
        name="deltanet_prep",
    )(qk, qk, v, ba, bat, prow, pcol)

    egl = eg[:, heads:, c_len - 1].reshape(n_chunks * heads)

    scan_rows = scan_chunks * c_len

    def head_major(last):
        return pl.BlockSpec((heads, scan_rows, last), lambda t: (0, t, 0))

    nat = pl.BlockSpec((scan_rows, width), lambda t: (t, 0))
    return pl.pallas_call(
        functools.partial(_dn_scan_kernel, heads=heads, chunks=scan_chunks),
        grid=(s_len // scan_rows,),
        in_specs=[pl.BlockSpec(memory_space=pltpu.SMEM),
                  head_major(HEAD_DIM), head_major(HEAD_DIM), head_major(HEAD_DIM),
                  head_major(HEAD_DIM), head_major(c_len), nat,
                  pl.BlockSpec((1, HEAD_DIM), lambda t: (0, 0))],
        out_specs=nat,
        out_shape=jax.ShapeDtypeStruct((s_len, width), BF16),
        scratch_shapes=[pltpu.VMEM((heads, HEAD_DIM, HEAD_DIM), F32)],
        compiler_params=_params(1),
        name="deltanet_scan",
    )(egl, u, w, qg, kd, intra, z, dn_norm.reshape(1, HEAD_DIM).astype(F32))


def _proj_norm_kernel(p_ref, w_ref, g_ref, o_ref, wcast_ref):
    @pl.when(pl.program_id(0) == 0)
    def _():
        wcast_ref[...] = w_ref[...].astype(BF16)

    y = jnp.dot(p_ref[...].astype(BF16), wcast_ref[...], preferred_element_type=F32)
    ms = jnp.mean(y * y, axis=-1, keepdims=True)
    o_ref[...] = (y * lax.rsqrt(ms + EPS) * g_ref[...]).astype(o_ref.dtype)


def proj_norm(p, w, g, out_dtype, *, tm=512):
    m, kdim = p.shape
    n = w.shape[1]
    return pl.pallas_call(
        _proj_norm_kernel,
        grid=(m // tm,),
        in_specs=[pl.BlockSpec((tm, kdim), lambda i: (i, 0)),
                  pl.BlockSpec((kdim, n), lambda i: (0, 0)),
                  pl.BlockSpec((1, n), lambda i: (0, 0))],
        out_specs=pl.BlockSpec((tm, n), lambda i: (i, 0)),
        out_shape=jax.ShapeDtypeStruct((m, n), out_dtype),
        scratch_shapes=[pltpu.VMEM((kdim, n), BF16)],
        compiler_params=_params(1),
        name="proj_norm",
    )(p, w, g.reshape(1, n).astype(F32))


def _layer(x, p, w_in, conv_w, a_log, dt_bias, dn_norm, w_attn_up, w_dn_up, w_out, w_mlp_up,
           w_mlp_down, w_ple_gate, w_ple_proj, norm_mix, norm_mlp, norm_ple, ple_post_norm):
    s_len, d_model = x.shape
    attn_w = ATTN_HEADS * HEAD_DIM
    dn_w = DN_HEADS * HEAD_DIM
    n_groups = len(ATTN_DILATIONS)
    c_qkv_b = 3 * n_groups * attn_w
    c_z = c_qkv_b + 3 * dn_w
    c_ba = c_z + dn_w
    c_gate = c_ba + 2 * DN_HEADS
    mm = matmul_ws

    w_in_t = w_in.T
    h = rmsnorm(x, norm_mix, BF16)
    qkv_a = mm(h, w_in_t, b_is_nk=True, col0=0, n=c_qkv_b, tn=1024, out_dtype=BF16,
               name="proj_attn")
    qk_b = mm(h, w_in_t, b_is_nk=True, col0=c_qkv_b, n=2 * dn_w, tn=1024, out_dtype=BF16,
              epilogue="conv_silu_l2norm", extra=((conv_w, 0),), name="proj_dn_qk")
    v_b = mm(h, w_in_t, b_is_nk=True, col0=c_qkv_b + 2 * dn_w, n=dn_w, tn=1024, out_dtype=BF16,
             epilogue="conv_silu", extra=((conv_w, 2 * dn_w),), name="proj_dn_v")
    z_b = mm(h, w_in_t, b_is_nk=True, col0=c_z, n=dn_w, tn=1024, out_dtype=BF16, name="proj_z")
    ba = mm(h, w_in_t, b_is_nk=True, col0=c_ba, n=LANES, tn=LANES, out_dtype=F32, name="proj_ba")
    gates = mm(h, w_in_t, b_is_nk=True, col0=c_gate, n=2 * d_model, tn=1024, out_dtype=BF16,
               epilogue="sigmoid", name="proj_gates")

    max_dil = max(ATTN_DILATIONS)
    per = s_len // max_dil

    def by_residue(cols, planes):
        t = cols.reshape(per, max_dil, cols.shape[-1]).transpose(1, 0, 2)
        return t.reshape(planes, max_dil // planes, per, cols.shape[-1])

    def natural(t):
        return t.transpose(2, 0, 1, 3).reshape(s_len, t.shape[-1])

    outs, lses = [], []
    for gi, dil in enumerate(ATTN_DILATIONS):
        if dil == 1:
            arr = qkv_a.reshape(1, 1, s_len, c_qkv_b)
            o_g, lse_g = banded_attention(
                arr, arr, arr, col_blocks=(gi, n_groups + gi, 2 * n_groups + gi))
        else:
            q_g, k_g, v_g = [
                by_residue(qkv_a[:, (t * n_groups + gi) * attn_w:(t * n_groups + gi + 1) * attn_w],
                           max_dil // dil) for t in range(3)]
            o_g, lse_g = banded_attention(q_g, k_g, v_g)
        outs.append(natural(o_g))
        lses.append(natural(lse_g))
    o_a = attn_combine(outs, lses)

    o_b = gated_deltanet2(qk_b, v_b, ba, a_log, dt_bias, z_b, dn_norm)

    t_a = mm(o_a, w_attn_up, tn=1024, out_dtype=BF16, epilogue="mul", extra=((gates, 0),),
             name="attn_up")
    merged = mm(o_b, w_dn_up, tn=1024, out_dtype=BF16, epilogue="mul_add",
                extra=((gates, d_model), (t_a, 0)), name="dn_up")
    x = mm(merged, w_out, tm=512, tn=1024, out_dtype=F32, epilogue="residual", extra=((x, 0),),
           name="out_proj")

    h = rmsnorm(x, norm_mlp, BF16)
    u = mm(h, w_mlp_up, tn=1024, out_dtype=BF16, epilogue="relu2", name="mlp_up")
    x = mm(u, w_mlp_down, out_dtype=F32, epilogue="residual", extra=((x, 0),), tk=4096,
           name="mlp_down")

    h = rmsnorm(x, norm_ple, BF16)
    pp = proj_norm(p, w_ple_proj, ple_post_norm, F32)
    x = mm(h, w_ple_gate, tm=512, tn=1024, out_dtype=F32, epilogue="ple",
           extra=((x, 0), (pp, 0)), name="ple_gate")
    return x


def kernel(x, p, w_in, conv_w, dn_a_log, dn_dt_bias, dn_norm, w_attn_up, w_dn_up, w_out,
           w_mlp_up, w_mlp_down, w_ple_gate, w_ple_proj, norm_mix, norm_mlp, norm_ple,
           ple_post_norm, final_norm):
    b, s_len, d_model = x.shape
    assert b == 1
    depth = w_in.shape[0]
    xs = x.reshape(s_len, d_model)
    for i in range(depth):
        xs = _layer(xs, p[i, 0], w_in[i], conv_w[i], dn_a_log[i], dn_dt_bias[i], dn_norm[i],
                    w_attn_up[i], w_dn_up[i], w_out[i], w_mlp_up[i], w_mlp_down[i],
                    w_ple_gate[i], w_ple_proj[i], norm_mix[i], norm_mlp[i], norm_ple[i],
                    ple_post_norm[i])
    out = rmsnorm(xs, final_norm, x.dtype)
    return out.reshape(b, s_len, d_model)
```

```python
import functools
import math

import jax
import jax.numpy as jnp
from jax import lax
from jax.experimental import pallas as pl
from jax.experimental.pallas import tpu as pltpu

F32 = jnp.float32
BF16 = jnp.bfloat16

EPS = 1e-6
LANES = 128
SUBLANES = 8
MXU_WIDTH = 256
HEAD_DIM = 128
ATTN_SPAN = 128
ATTN_DILATIONS = (1, 4, 16)
ATTN_HEADS = 8
DN_HEADS = 16
DN_CHUNK = 64
CONV_K = 4

VMEM_LIMIT_BYTES = 56 * 1024 * 1024


def _params(n_axes):
    return pltpu.CompilerParams(
        dimension_semantics=("arbitrary",) * n_axes,
        vmem_limit_bytes=VMEM_LIMIT_BYTES,
    )


def _rmsnorm_kernel(x_ref, g_ref, o_ref):
    x = x_ref[...].astype(F32)
    ms = jnp.mean(x * x, axis=-1, keepdims=True)
    o_ref[...] = (x * lax.rsqrt(ms + EPS) * g_ref[...]).astype(o_ref.dtype)


def rmsnorm(x, g, out_dtype, *, tm=256):
    m, d = x.shape
    return pl.pallas_call(
        _rmsnorm_kernel,
        grid=(m // tm,),
        in_specs=[pl.BlockSpec((tm, d), lambda i: (i, 0)),
                  pl.BlockSpec((1, d), lambda i: (0, 0))],
        out_specs=pl.BlockSpec((tm, d), lambda i: (i, 0)),
        out_shape=jax.ShapeDtypeStruct((m, d), out_dtype),
        compiler_params=_params(1),
        name="rmsnorm",
    )(x, g.reshape(1, d).astype(F32))


def _sigmoid(x):
    return 1.0 / (1.0 + jnp.exp(-x))


def _mm_kernel(*refs, nk, tm, epilogue, n_extra, b_is_nk):
    a_ref, bchunk_ref = refs[0], refs[1]
    extra = refs[2:2 + n_extra]
    o_ref = refs[2 + n_extra]
    bcast_ref = refs[3 + n_extra]
    acc_ref = refs[4 + n_extra]
    t = pl.program_id(0)
    i = pl.program_id(1)
    chunk_rows = bchunk_ref.shape[0]

    def stage():
        rows = pl.ds(pl.multiple_of(i * chunk_rows, chunk_rows), chunk_rows)
        bcast_ref[t % 2, rows, :] = bchunk_ref[...].astype(BF16)

    @pl.when(t == 0)
    def _():
        stage()

    @pl.when(t > 0)
    def _():
        stage()
        _mm_compute(a_ref, bcast_ref, extra, o_ref, acc_ref, t - 1, i, nk=nk, tm=tm,
                    epilogue=epilogue, b_is_nk=b_is_nk)


def _mm_compute(a_ref, bcast_ref, extra, o_ref, acc_ref, tile, i, *, nk, tm, epilogue, b_is_nk):
    k = tile % nk
    slot = tile % 2
    tn = o_ref.shape[-1]
    strip = min(MXU_WIDTH, tn)
    rows = pl.ds(pl.multiple_of(i * tm, tm), tm)
    conv = epilogue.startswith("conv_silu")
    raw_rows = slice(SUBLANES, SUBLANES + tm) if conv else slice(0, tm)
    if conv:

        @pl.when(i == 0)
        def _():
            acc_ref[0:SUBLANES, :] = jnp.zeros((SUBLANES, tn), F32)

    for c in range(tn // strip):
        cols = slice(c * strip, (c + 1) * strip)
        if b_is_nk:
            acc = lax.dot_general(a_ref[...], bcast_ref[slot, cols, :], (((1,), (1,)), ((), ())),
                                  preferred_element_type=F32)
        else:
            acc = jnp.dot(a_ref[...], bcast_ref[slot, :, cols], preferred_element_type=F32)
        if nk > 1:
            acc_ref[rows, cols] = acc + jnp.where(k == 0, 0.0, acc_ref[rows, cols])
        elif epilogue == "plain_by_residue":
            for piece in range(strip // LANES):
                acc_ref[c * (strip // LANES) + piece] = acc[:, piece * LANES:(piece + 1) * LANES]
        else:
            acc_ref[raw_rows, cols] = acc
    if epilogue == "plain_by_residue":
        n_res = o_ref.shape[0]
        for piece in range(tn // LANES):
            for r in range(n_res):
                o_ref[r, :, piece * LANES:(piece + 1) * LANES] = acc_ref.at[piece][
                    pl.ds(r, tm // n_res, stride=n_res), :].astype(o_ref.dtype)
        return
    for c in range(tn // strip):
        cols = slice(c * strip, (c + 1) * strip)
        acc = acc_ref[rows, cols] if nk > 1 else acc_ref[raw_rows, cols]

        if epilogue == "plain":
            val = acc
        elif epilogue == "relu2":
            r = jnp.maximum(acc, 0.0)
            val = r * r
        elif epilogue == "sigmoid":
            val = _sigmoid(acc)
        elif epilogue == "residual":
            val = extra[0][:, cols].astype(F32) + acc
        elif epilogue == "mul":
            val = extra[0][:, cols].astype(F32) * acc
        elif epilogue == "mul_add":
            val = extra[1][:, cols].astype(F32) + extra[0][:, cols].astype(F32) * acc
        elif epilogue == "ple":
            val = extra[0][:, cols].astype(F32) + _sigmoid(acc) * extra[1][:, cols].astype(F32)
        elif epilogue.startswith("conv_silu"):
            halo = SUBLANES
            cw = extra[0][:, cols]
            base = halo - (CONV_K - 1)
            y = acc * cw[CONV_K - 1:CONV_K]
            for tap in range(CONV_K - 1):
                y = y + acc_ref[base + tap:base + tap + tm, cols] * cw[tap:tap + 1]
            acc_ref[0:halo, cols] = acc[tm - halo:, :]
            y = y * _sigmoid(y)
            if epilogue == "conv_silu_l2norm":
                segs = []
                for hh in range(strip // HEAD_DIM):
                    seg = y[:, hh * HEAD_DIM:(hh + 1) * HEAD_DIM]
                    segs.append(seg * lax.rsqrt(jnp.sum(seg * seg, axis=-1, keepdims=True) + EPS))
                y = jnp.concatenate(segs, axis=1)
            val = y
        else:
            raise ValueError(epilogue)
        o_ref[:, cols] = val.astype(o_ref.dtype)


def matmul_ws(a, b, *, col0=0, n=None, out_dtype=BF16, epilogue="plain", extra=(),
              tm=1024, tn=512, tk=None, b_is_nk=False, panel_col=None, residues=1,
              name="matmul_ws"):
    m, kdim = a.shape
    n_total = b.shape[0] if b_is_nk else b.shape[1]
    if n is None:
        n = n_total - col0
    if tk is None:
        tk = kdim
    assert m % tm == 0 and n % tn == 0 and kdim % tk == 0 and col0 + n <= n_total
    assert panel_col is None or b_is_nk
    nj, nk, ni = n // tn, kdim // tk, m // tm
    n_tiles = nj * nk

    def tile_jk(t):
        tile = jnp.maximum(t - 1, 0)
        return tile // nk, tile % nk

    def row_block(t, i):
        _, k = tile_jk(t)
        return jnp.where(jnp.logical_and(t > 0, k == nk - 1), i, 0)

    def a_map(t, i):
        return (jnp.where(t > 0, i, 0), tile_jk(t)[1])

    def out_map(t, i):
        return (row_block(t, i), tile_jk(t)[0])

    def staged(t, i):
        tile = jnp.minimum(t, n_tiles - 1)
        return tile // nk, tile % nk, jnp.where(t == n_tiles, ni - 1, i)

    if b_is_nk:
        assert col0 % SUBLANES == 0 and tn % (ni * SUBLANES) == 0
        cn = tn // ni

        def b_map(t, i):
            j, k, c = staged(t, i)
            first = col0 + j * tn if panel_col is None else panel_col(j)
            return (pl.multiple_of(first + c * cn, SUBLANES), pl.multiple_of(k * tk, LANES))

        b_spec = pl.BlockSpec((pl.Element(cn), pl.Element(tk)), b_map)
        bcast_shape = (2, tn, tk)
    else:
        assert col0 % tn == 0 and tk % (ni * SUBLANES) == 0

        def b_map(t, i):
            j, k, c = staged(t, i)
            return (k * ni + c, j + col0 // tn)

        b_spec = pl.BlockSpec((tk // ni, tn), b_map)
        bcast_shape = (2, tk, tn)
    in_specs = [pl.BlockSpec((tm, tk), a_map), b_spec]
    conv = epilogue.startswith("conv_silu")
    for idx, (_, ecol0) in enumerate(extra):
        assert ecol0 % tn == 0
        if conv and idx == 0:
            in_specs.append(pl.BlockSpec(
                (CONV_K, tn),
                functools.partial(lambda t, i, e0: (0, tile_jk(t)[0] + e0), e0=ecol0 // tn)))
        else:
            in_specs.append(pl.BlockSpec(
                (tm, tn),
                functools.partial(lambda t, i, e0: (row_block(t, i), tile_jk(t)[0] + e0),
                                  e0=ecol0 // tn)))
    if epilogue == "plain_by_residue":
        assert nk == 1 and tm % (residues * SUBLANES) == 0
        out_spec = pl.BlockSpec((residues, tm // residues, tn),
                                lambda t, i: (0, row_block(t, i), tile_jk(t)[0]))
        out_shape = jax.ShapeDtypeStruct((residues, m // residues, n), out_dtype)
    else:
        out_spec = pl.BlockSpec((tm, tn), out_map)
        out_shape = jax.ShapeDtypeStruct((m, n), out_dtype)
    scratch = [pltpu.VMEM(bcast_shape, BF16)]
    if conv:
        assert nk == 1
        scratch.append(pltpu.VMEM((SUBLANES + tm, tn), F32))
    elif epilogue == "plain_by_residue":
        scratch.append(pltpu.VMEM((tn // LANES, tm, LANES), F32))
    else:
        scratch.append(pltpu.VMEM((m if nk > 1 else tm, tn), F32))
    return pl.pallas_call(
        functools.partial(_mm_kernel, nk=nk, tm=tm, epilogue=epilogue, n_extra=len(extra),
                          b_is_nk=b_is_nk),
        grid=(n_tiles + 1, ni),
        in_specs=in_specs,
        out_specs=out_spec,
        out_shape=out_shape,
        scratch_shapes=scratch,
        compiler_params=_params(2),
        name=name,
    )(a, b, *[arr for arr, _ in extra])


def _attn_kernel(q_ref, kp_ref, kc_ref, vp_ref, vc_ref, o_ref, lse_ref, *, heads, scale):
    n = pl.program_id(1)
    planes, sub, _ = q_ref.shape
    span = planes * sub
    rho = lax.broadcasted_iota(jnp.int32, (span, 2 * span), 0)
    kap = lax.broadcasted_iota(jnp.int32, (span, 2 * span), 1)
    in_cur = kap // span
    dist = (planes * (sub * (1 - in_cur) + rho % sub - kap % sub)
            + rho // sub - (kap % span) // sub)
    valid = (dist >= 0) & (dist <= span) & ((in_cur == 1) | (n > 0))
    lane = lax.broadcasted_iota(jnp.int32, (span, HEAD_DIM), 1)
    lse_tile = jnp.zeros((span, HEAD_DIM), F32)

    def tile(ref, cols):
        return jnp.concatenate([ref[a, :, cols] for a in range(planes)], axis=0)

    for h in range(heads):
        cols = slice(h * HEAD_DIM, (h + 1) * HEAD_DIM)
        q = tile(q_ref, cols)
        k = jnp.concatenate([tile(kp_ref, cols), tile(kc_ref, cols)], axis=0)
        v = jnp.concatenate([tile(vp_ref, cols), tile(vc_ref, cols)], axis=0)
        s = lax.dot_general(q, k, (((1,), (1,)), ((), ())), preferred_element_type=F32) * scale
        s = jnp.where(valid, s, -jnp.inf)
        m = jnp.max(s, axis=-1, keepdims=True)
        e = jnp.exp(s - m)
        l = jnp.sum(e, axis=-1, keepdims=True)
        o = (jnp.dot(e.astype(BF16), v, preferred_element_type=F32) / l).astype(o_ref.dtype)
        for a in range(planes):
            o_ref[a, :, cols] = o[a * sub:(a + 1) * sub]
        lse_tile = jnp.where(lane == h, m + jnp.log(l), lse_tile)
    for a in range(planes):
        lse_ref[a] = lse_tile[a * sub:(a + 1) * sub]


def banded_attention(q, k, v, *, heads=ATTN_HEADS, col_blocks=(0, 0, 0)):
    planes, n_sub, rows, _ = q.shape
    width = heads * HEAD_DIM
    sub = ATTN_SPAN // planes
    nb = rows // sub

    def cur_spec(cb):
        return pl.BlockSpec((planes, None, sub, width), lambda r, n: (0, r, n, cb))

    def prev_spec(cb):
        return pl.BlockSpec((planes, None, sub, width),
                            lambda r, n: (0, r, jnp.maximum(n - 1, 0), cb))

    qc, kc, vc = col_blocks
    return pl.pallas_call(
        functools.partial(_attn_kernel, heads=heads, scale=HEAD_DIM ** -0.5),
        grid=(n_sub, nb),
        in_specs=[cur_spec(qc), prev_spec(kc), cur_spec(kc), prev_spec(vc), cur_spec(vc)],
        out_specs=[pl.BlockSpec((planes, None, sub, width), lambda r, n: (0, r, n, 0)),
                   pl.BlockSpec((planes, None, sub, HEAD_DIM), lambda r, n: (0, r, n, 0))],
        out_shape=[jax.ShapeDtypeStruct((planes, n_sub, rows, width), BF16),
                   jax.ShapeDtypeStruct((planes, n_sub, rows, HEAD_DIM), F32)],
        compiler_params=_params(2),
        name="banded_attention",
    )(q, k, k, v, v)


def _attn_combine_kernel(o0_ref, o1_ref, o2_ref, l0_ref, l1_ref, l2_ref, out_ref, *, heads):
    l0, l1, l2 = l0_ref[...], l1_ref[...], l2_ref[...]
    m = jnp.maximum(jnp.maximum(l0, l1), l2)
    e0, e1, e2 = jnp.exp(l0 - m), jnp.exp(l1 - m), jnp.exp(l2 - m)
    den = e0 + e1 + e2
    w0, w1, w2 = e0 / den, e1 / den, e2 / den
    for h in range(heads):
        cols = slice(h * HEAD_DIM, (h + 1) * HEAD_DIM)
        acc = (w0[:, h:h + 1] * o0_ref[:, cols] + w1[:, h:h + 1] * o1_ref[:, cols]
               + w2[:, h:h + 1] * o2_ref[:, cols])
        out_ref[:, cols] = acc.astype(out_ref.dtype)


def attn_combine(outs, lses, *, tm=256):
    m, width = outs[0].shape
    heads = width // HEAD_DIM
    ospec = pl.BlockSpec((tm, width), lambda i: (i, 0))
    lspec = pl.BlockSpec((tm, HEAD_DIM), lambda i: (i, 0))
    return pl.pallas_call(
        functools.partial(_attn_combine_kernel, heads=heads),
        grid=(m // tm,),
        in_specs=[ospec] * 3 + [lspec] * 3,
        out_specs=ospec,
        out_shape=jax.ShapeDtypeStruct((m, width), BF16),
        compiler_params=_params(1),
        name="attn_combine",
    )(*outs, *lses)


def _softplus(x):
    return jnp.maximum(x, 0.0) + jnp.log(1.0 + jnp.exp(-jnp.abs(x)))


def _bdot(a, b, dims):
    return lax.dot_general(a, b, (dims, ((0,), (0,))), preferred_element_type=F32)


def _bmm(a, b):
    return _bdot(a, b, ((2,), (1,)))


def _split_bf16(x):
    hi = x.astype(BF16)
    lo = (x - hi.astype(F32)).astype(BF16)
    return hi, lo


def _bmm_split(a, b):
    a_hi, a_lo = _split_bf16(a)
    b_hi, b_lo = _split_bf16(b)
    return _bmm(a_hi, b_hi) + _bmm(a_hi, b_lo) + _bmm(a_lo, b_hi)


def _unit_lower_inverse_batched(a):
    c = a.shape[-1]
    b0 = SUBLANES
    ri = lax.broadcasted_iota(jnp.int32, (c, c), 0)
    ci = lax.broadcasted_iota(jnp.int32, (c, c), 1)
    below = ri > ci
    eye = (ri == ci).astype(F32)

    def same_block(b):
        return (ri // b) == (ci // b)

    def mm(x, y):
        return _bmm(x.astype(BF16), y.astype(BF16))

    d = jnp.where(same_block(b0) & below, a, 0.0)
    p = eye - d
    x = d
    for _ in range(int(math.log2(b0)) - 1):
        x = mm(x, x)
        p = p + mm(p, x)
    b = b0
    while b < c:
        off_diag = jnp.where(same_block(2 * b) & jnp.logical_not(same_block(b)) & below, a, 0.0)
        p = p - mm(mm(p, off_diag), p)
        b *= 2
    resid = eye - p - _bmm_split(a, p)
    return p + mm(p, resid)


def _dn_prep_kernel(q_ref, k_ref, v_ref, ba_ref, bat_ref, prow_ref, pcol_ref,
                    u_ref, w_ref, qg_ref, kd_ref, intra_ref, eg_ref, *, heads, chunks):
    c_len = DN_CHUNK

    ba = ba_ref[...]
    beta_col = _sigmoid(ba)
    g_col = -jnp.exp(prow_ref[0:1, :]) * _softplus(ba + prow_ref[1:2, :])
    row_in_chunk = lax.broadcasted_iota(jnp.int32, g_col.shape, 0) % c_len
    shift = 1
    while shift < c_len:
        g_col = g_col + jnp.where(row_in_chunk >= shift, pltpu.roll(g_col, shift, 0), 0.0)
        shift *= 2
    bat = bat_ref[...]
    beta_row = _sigmoid(bat)
    g_row = -jnp.exp(pcol_ref[:, 0:1])[None] * _softplus(bat + pcol_ref[:, 1:2][None])
    lane_in_chunk = lax.broadcasted_iota(jnp.int32, g_row.shape, 2)
    shift = 1
    while shift < c_len:
        g_row = g_row + jnp.where(lane_in_chunk >= shift, pltpu.roll(g_row, shift, 2), 0.0)
        shift *= 2
    eg_ref[...] = jnp.exp(g_row)

    ri = lax.broadcasted_iota(jnp.int32, (c_len, c_len), 0)
    ci = lax.broadcasted_iota(jnp.int32, (c_len, c_len), 1)
    lower = ri >= ci
    strict = ri > ci

    units = [(c, h) for c in range(chunks) for h in range(heads)]

    def rows_of(c):
        return slice(c * c_len, (c + 1) * c_len)

    def per_unit(x_ref):
        return jnp.stack([x_ref[rows_of(c), h * HEAD_DIM:(h + 1) * HEAD_DIM] for c, h in units])

    q = per_unit(q_ref).astype(F32) * HEAD_DIM ** -0.5
    k16 = per_unit(k_ref)
    k = k16.astype(F32)
    v16 = per_unit(v_ref)
    gcol = jnp.stack([g_col[rows_of(c), heads + h:heads + h + 1] for c, h in units])
    bcol = jnp.stack([beta_col[rows_of(c), h:h + 1] for c, h in units])
    grow = jnp.stack([g_row[c, heads + h:heads + h + 1, :] for c, h in units])
    brow = jnp.stack([beta_row[c, h:h + 1, :] for c, h in units])
    decay = jnp.exp(jnp.where(lower, gcol - grow, -jnp.inf))
    q16 = q.astype(BF16)
    qk = _bdot(jnp.concatenate([q16, k16], axis=1), k16, ((2,), (2,)))
    intra = qk[:, :c_len] * decay
    a = jnp.where(strict, qk[:, c_len:] * decay * bcol, 0.0)
    t_inv = _unit_lower_inverse_batched(a)
    u = _bmm((t_inv * brow).astype(BF16), v16)
    w = _bmm((t_inv * (brow * jnp.exp(grow))).astype(BF16), k16)
    g_last = grow[:, :, c_len - 1:c_len]
    qg = q * jnp.exp(gcol)
    kd = k * jnp.exp(g_last - gcol)
    for c in range(chunks):
        rs, sl = rows_of(c), slice(c * heads, (c + 1) * heads)
        u_ref[:, rs, :] = u[sl]
        w_ref[:, rs, :] = w[sl].astype(w_ref.dtype)
        qg_ref[:, rs, :] = qg[sl].astype(qg_ref.dtype)
        kd_ref[:, rs, :] = kd[sl].astype(kd_ref.dtype)
        intra_ref[:, rs, :] = intra[sl].astype(intra_ref.dtype)


def _dn_scan_kernel(egl_ref, u_ref, w_ref, qg_ref, kd_ref, intra_ref, z_ref, gn_ref, o_ref,
                    state_ref, *, heads, chunks):
    tb = pl.program_id(0)
    c_len = DN_CHUNK

    @pl.when(tb == 0)
    def _():
        state_ref[...] = jnp.zeros_like(state_ref)

    gn = gn_ref[...]
    for c in range(chunks):
        rs = slice(c * c_len, (c + 1) * c_len)
        s = state_ref[...]
        lhs = jnp.concatenate([w_ref[:, rs, :], qg_ref[:, rs, :]], axis=1)
        r = _bmm(lhs, s.astype(BF16))
        v_new = u_ref[:, rs, :] - r[:, :c_len]
        vn16 = v_new.astype(BF16)
        o = r[:, c_len:] + _bmm(intra_ref[:, rs, :], vn16)
        upd = _bdot(kd_ref[:, rs, :], vn16, ((1,), (1,)))
        base = (tb * chunks + c) * heads
        for h in range(heads):
            state_ref[h] = s[h] * egl_ref[base + h] + upd[h]
        o = o * lax.rsqrt(jnp.mean(o * o, axis=-1, keepdims=True) + EPS) * gn
        for h in range(heads):
            cs = slice(h * HEAD_DIM, (h + 1) * HEAD_DIM)
            zz = z_ref[rs, cs].astype(F32)
            o_ref[rs, cs] = (o[h] * (zz * _sigmoid(zz))).astype(o_ref.dtype)


def gated_deltanet2(qk, v, ba, a_log, dt_bias, z, dn_norm, *, prep_chunks=4, scan_chunks=4):
    s_len, width = z.shape
    heads = width // HEAD_DIM
    c_len = DN_CHUNK
    n_chunks = s_len // c_len
    bat = ba[:, :2 * heads].reshape(n_chunks, c_len, 2 * heads).transpose(0, 2, 1)
    prow = jnp.zeros((2, HEAD_DIM), F32)
    prow = prow.at[0, heads:2 * heads].set(a_log.astype(F32))
    prow = prow.at[1, heads:2 * heads].set(dt_bias.astype(F32))
    pcol = jnp.zeros((2 * heads, 2), F32)
    pcol = pcol.at[heads:, 0].set(a_log.astype(F32)).at[heads:, 1].set(dt_bias.astype(F32))

    rows = prep_chunks * c_len

    def main_spec(i):
        return pl.BlockSpec((rows, width), lambda t: (t, i))

    def prep_out(last):
        return pl.BlockSpec((heads, rows, last), lambda t: (0, t, 0))

    u, w, qg, kd, intra, eg = pl.pallas_call(
        functools.partial(_dn_prep_kernel, heads=heads, chunks=prep_chunks),
        grid=(s_len // rows,),
        in_specs=[main_spec(0), main_spec(1), main_spec(0),
                  pl.BlockSpec((rows, HEAD_DIM), lambda t: (t, 0)),
                  pl.BlockSpec((prep_chunks, 2 * heads, c_len), lambda t: (t, 0, 0)),
                  pl.BlockSpec((2, HEAD_DIM), lambda t: (0, 0)),
                  pl.BlockSpec((2 * heads, 2), lambda t: (0, 0))],
        out_specs=[prep_out(HEAD_DIM), prep_out(HEAD_DIM), prep_out(HEAD_DIM),
                   prep_out(HEAD_DIM), prep_out(c_len),
                   pl.BlockSpec((prep_chunks, 2 * heads, c_len), lambda t: (t, 0, 0))],
        out_shape=[jax.ShapeDtypeStruct((heads, s_len, HEAD_DIM), F32),
                   jax.ShapeDtypeStruct((heads, s_len, HEAD_DIM), BF16),
                   jax.ShapeDtypeStruct((heads, s_len, HEAD_DIM), BF16),
                   jax.ShapeDtypeStruct((heads, s_len, HEAD_DIM), BF16),
                   jax.ShapeDtypeStruct((heads, s_len, c_len), BF16),
                   jax.ShapeDtypeStruct((n_chunks, 2 * heads, c_len), F32)],
        compiler_params=_params(1),
        name="deltanet_prep",
    )(qk, qk, v, ba, bat, prow, pcol)

    egl = eg[:, heads:, c_len - 1].reshape(n_chunks * heads)

    scan_rows = scan_chunks * c_len

    def head_major(last):
        return pl.BlockSpec((heads, scan_rows, last), lambda t: (0, t, 0))

    nat = pl.BlockSpec((scan_rows, width), lambda t: (t, 0))
    return pl.pallas_call(
        functools.partial(_dn_scan_kernel, heads=heads, chunks=scan_chunks),
        grid=(s_len // scan_rows,),
        in_specs=[pl.BlockSpec(memory_space=pltpu.SMEM),
                  head_major(HEAD_DIM), head_major(HEAD_DIM), head_major(HEAD_DIM),
                  head_major(HEAD_DIM), head_major(c_len), nat,
                  pl.BlockSpec((1, HEAD_DIM), lambda t: (0, 0))],
        out_specs=nat,
        out_shape=jax.ShapeDtypeStruct((s_len, width), BF16),
        scratch_shapes=[pltpu.VMEM((heads, HEAD_DIM, HEAD_DIM), F32)],
        compiler_params=_params(1),
        name="deltanet_scan",
    )(egl, u, w, qg, kd, intra, z, dn_norm.reshape(1, HEAD_DIM).astype(F32))


def _proj_norm_kernel(p_ref, w_ref, g_ref, o_ref, wcast_ref):
    @pl.when(pl.program_id(0) == 0)
    def _():
        wcast_ref[...] = w_ref[...].astype(BF16)

    y = jnp.dot(p_ref[...].astype(BF16), wcast_ref[...], preferred_element_type=F32)
    ms = jnp.mean(y * y, axis=-1, keepdims=True)
    o_ref[...] = (y * lax.rsqrt(ms + EPS) * g_ref[...]).astype(o_ref.dtype)


def proj_norm(p, w, g, out_dtype, *, tm=512):
    m, kdim = p.shape
    n = w.shape[1]
    return pl.pallas_call(
        _proj_norm_kernel,
        grid=(m // tm,),
        in_specs=[pl.BlockSpec((tm, kdim), lambda i: (i, 0)),
                  pl.BlockSpec((kdim, n), lambda i: (0, 0)),
                  pl.BlockSpec((1, n), lambda i: (0, 0))],
        out_specs=pl.BlockSpec((tm, n), lambda i: (i, 0)),
        out_shape=jax.ShapeDtypeStruct((m, n), out_dtype),
        scratch_shapes=[pltpu.VMEM((kdim, n), BF16)],
        compiler_params=_params(1),
        name="proj_norm",
    )(p, w, g.reshape(1, n).astype(F32))


def _layer(x, p, w_in, conv_w, a_log, dt_bias, dn_norm, w_attn_up, w_dn_up, w_out, w_mlp_up,
           w_mlp_down, w_ple_gate, w_ple_proj, norm_mix, norm_mlp, norm_ple, ple_post_norm):
    s_len, d_model = x.shape
    attn_w = ATTN_HEADS * HEAD_DIM
    dn_w = DN_HEADS * HEAD_DIM
    n_groups = len(ATTN_DILATIONS)
    c_qkv_b = 3 * n_groups * attn_w
    c_z = c_qkv_b + 3 * dn_w
    c_ba = c_z + dn_w
    c_gate = c_ba + 2 * DN_HEADS
    mm = matmul_ws

    w_in_t = w_in.T
    h = rmsnorm(x, norm_mix, BF16)
    max_dil = max(ATTN_DILATIONS)
    qkv_g0 = mm(h, w_in_t, b_is_nk=True, n=3 * attn_w, tn=attn_w, out_dtype=BF16,
                panel_col=lambda j: j * (n_groups * attn_w), name="proj_attn_g0")
    qkv_g12 = mm(h, w_in_t, b_is_nk=True, n=6 * attn_w, tn=attn_w, out_dtype=BF16,
                 epilogue="plain_by_residue", residues=max_dil,
                 panel_col=lambda j: (j // 2) * (n_groups * attn_w) + (1 + j % 2) * attn_w,
                 name="proj_attn_g12")
    qk_b = mm(h, w_in_t, b_is_nk=True, col0=c_qkv_b, n=2 * dn_w, tn=1024, out_dtype=BF16,
              epilogue="conv_silu_l2norm", extra=((conv_w, 0),), name="proj_dn_qk")
    v_b = mm(h, w_in_t, b_is_nk=True, col0=c_qkv_b + 2 * dn_w, n=dn_w, tn=1024, out_dtype=BF16,
             epilogue="conv_silu", extra=((conv_w, 2 * dn_w),), name="proj_dn_v")
    z_b = mm(h, w_in_t, b_is_nk=True, col0=c_z, n=dn_w, tn=1024, out_dtype=BF16, name="proj_z")
    ba = mm(h, w_in_t, b_is_nk=True, col0=c_ba, n=LANES, tn=LANES, out_dtype=F32, name="proj_ba")
    gates = mm(h, w_in_t, b_is_nk=True, col0=c_gate, n=2 * d_model, tn=1024, out_dtype=BF16,
               epilogue="sigmoid", name="proj_gates")

    g0 = qkv_g0.reshape(1, 1, s_len, 3 * attn_w)
    o0, lse0 = banded_attention(g0, g0, g0, col_blocks=(0, 1, 2))
    per = s_len // max_dil
    g1 = qkv_g12.reshape(4, 4, per, 6 * attn_w)
    o1, lse1 = banded_attention(g1, g1, g1, col_blocks=(0, 2, 4))
    g2 = qkv_g12.reshape(1, max_dil, per, 6 * attn_w)
    o2, lse2 = banded_attention(g2, g2, g2, col_blocks=(1, 3, 5))

    def natural(t):
        return t.transpose(2, 0, 1, 3).reshape(s_len, t.shape[-1])

    o_a = attn_combine([o0.reshape(s_len, attn_w), natural(o1), natural(o2)],
                       [lse0.reshape(s_len, HEAD_DIM), natural(lse1), natural(lse2)])

    o_b = gated_deltanet2(qk_b, v_b, ba, a_log, dt_bias, z_b, dn_norm)

    t_a = mm(o_a, w_attn_up, tn=1024, out_dtype=BF16, epilogue="mul", extra=((gates, 0),),
             name="attn_up")
    merged = mm(o_b, w_dn_up, tn=1024, out_dtype=BF16, epilogue="mul_add",
                extra=((gates, d_model), (t_a, 0)), name="dn_up")
    x = mm(merged, w_out, tm=512, tn=1024, out_dtype=F32, epilogue="residual", extra=((x, 0),),
           name="out_proj")

    h = rmsnorm(x, norm_mlp, BF16)
    u = mm(h, w_mlp_up, tn=1024, out_dtype=BF16, epilogue="relu2", name="mlp_up")
    x = mm(u, w_mlp_down, out_dtype=F32, epilogue="residual", extra=((x, 0),), tk=4096,
           name="mlp_down")

    h = rmsnorm(x, norm_ple, BF16)
    pp = proj_norm(p, w_ple_proj, ple_post_norm, F32)
    x = mm(h, w_ple_gate, tm=512, tn=1024, out_dtype=F32, epilogue="ple",
           extra=((x, 0), (pp, 0)), name="ple_gate")
    return x


def kernel(x, p, w_in, conv_w, dn_a_log, dn_dt_bias, dn_norm, w_attn_up, w_dn_up, w_out,
           w_mlp_up, w_mlp_down, w_ple_gate, w_ple_proj, norm_mix, norm_mlp, norm_ple,
           ple_post_norm, final_norm):
    b, s_len, d_model = x.shape
    assert b == 1
    depth = w_in.shape[0]
    xs = x.reshape(s_len, d_model)
    for i in range(depth):
        xs = _layer(xs, p[i, 0], w_in[i], conv_w[i], dn_a_log[i], dn_dt_bias[i], dn_norm[i],
                    w_attn_up[i], w_dn_up[i], w_out[i], w_mlp_up[i], w_mlp_down[i],
                    w_ple_gate[i], w_ple_proj[i], norm_mix[i], norm_mlp[i], norm_ple[i],
                    ple_post_norm[i])
    out = rmsnorm(xs, final_norm, x.dtype)
    return out.reshape(b, s_len, d_model)
```

```python
import functools
import math

import jax
import jax.numpy as jnp
from jax import lax
from jax.experimental import pallas as pl
from jax.experimental.pallas import tpu as pltpu

F32 = jnp.float32
BF16 = jnp.bfloat16

EPS = 1e-6
LANES = 128
SUBLANES = 8
MXU_WIDTH = 256
ROW_SUB = 256
HEAD_DIM = 128
ATTN_SPAN = 128
ATTN_DILATIONS = (1, 4, 16)
ATTN_HEADS = 8
DN_HEADS = 16
DN_CHUNK = 64
CONV_K = 4

VMEM_LIMIT_BYTES = 56 * 1024 * 1024


def _params(n_axes):
    return pltpu.CompilerParams(
        dimension_semantics=("arbitrary",) * n_axes,
        vmem_limit_bytes=VMEM_LIMIT_BYTES,
    )


def _rmsnorm_kernel(x_ref, g_ref, o_ref):
    x = x_ref[...].astype(F32)
    ms = jnp.mean(x * x, axis=-1, keepdims=True)
    o_ref[...] = (x * lax.rsqrt(ms + EPS) * g_ref[...]).astype(o_ref.dtype)


def rmsnorm(x, g, out_dtype, *, tm=256):
    m, d = x.shape
    return pl.pallas_call(
        _rmsnorm_kernel,
        grid=(m // tm,),
        in_specs=[pl.BlockSpec((tm, d), lambda i: (i, 0)),
                  pl.BlockSpec((1, d), lambda i: (0, 0))],
        out_specs=pl.BlockSpec((tm, d), lambda i: (i, 0)),
        out_shape=jax.ShapeDtypeStruct((m, d), out_dtype),
        compiler_params=_params(1),
        name="rmsnorm",
    )(x, g.reshape(1, d).astype(F32))


def _sigmoid(x):
    return 1.0 / (1.0 + jnp.exp(-x))


def _mm_kernel(*refs, nk, tm, epilogue, n_extra, b_is_nk):
    a_ref, bchunk_ref = refs[0], refs[1]
    extra = refs[2:2 + n_extra]
    o_ref = refs[2 + n_extra]
    w_even, w_odd = refs[3 + n_extra], refs[4 + n_extra]
    acc_ref = refs[5 + n_extra]
    t = pl.program_id(0)
    i = pl.program_id(1)
    chunk_rows = bchunk_ref.shape[0]

    def stage(w_ref):
        rows = pl.ds(pl.multiple_of(i * chunk_rows, chunk_rows), chunk_rows)
        w_ref[rows, :] = bchunk_ref[...].astype(BF16)

    def compute(w_ref):
        _mm_compute(a_ref, w_ref, extra, o_ref, acc_ref, t - 1, i, nk=nk, tm=tm,
                    epilogue=epilogue, b_is_nk=b_is_nk)

    @pl.when(t == 0)
    def _():
        stage(w_even)

    @pl.when(jnp.logical_and(t > 0, t % 2 == 1))
    def _():
        stage(w_odd)
        compute(w_even)

    @pl.when(jnp.logical_and(t > 0, t % 2 == 0))
    def _():
        stage(w_even)
        compute(w_odd)


def _mm_compute(a_ref, w_ref, extra, o_ref, acc_ref, tile, i, *, nk, tm, epilogue, b_is_nk):
    k = tile % nk
    tn = o_ref.shape[-1]
    strip = min(MXU_WIDTH, tn)
    conv = epilogue.startswith("conv_silu")

    def product(row_slice, cols):
        if b_is_nk:
            return lax.dot_general(a_ref[row_slice, :], w_ref[cols, :], (((1,), (1,)), ((), ())),
                                   preferred_element_type=F32)
        return jnp.dot(a_ref[row_slice, :], w_ref[:, cols], preferred_element_type=F32)

    if not conv and epilogue != "plain_by_residue":
        row_sub = min(ROW_SUB, tm)
        for rs in range(tm // row_sub):
            rr = slice(rs * row_sub, (rs + 1) * row_sub)
            for c in range(tn // strip):
                cols = slice(c * strip, (c + 1) * strip)
                acc = product(rr, cols)
                if nk > 1:
                    arows = pl.ds(pl.multiple_of(i * tm + rs * row_sub, row_sub), row_sub)
                    acc = acc + jnp.where(k == 0, 0.0, acc_ref[arows, cols])
                    acc_ref[arows, cols] = acc
                if epilogue == "plain":
                    val = acc
                elif epilogue == "relu2":
                    r = jnp.maximum(acc, 0.0)
                    val = r * r
                elif epilogue == "sigmoid":
                    val = _sigmoid(acc)
                elif epilogue == "residual":
                    val = extra[0][rr, cols].astype(F32) + acc
                elif epilogue == "mul":
                    val = extra[0][rr, cols].astype(F32) * acc
                elif epilogue == "mul_add":
                    val = extra[1][rr, cols].astype(F32) + extra[0][rr, cols].astype(F32) * acc
                elif epilogue == "ple":
                    val = (extra[0][rr, cols].astype(F32)
                           + _sigmoid(acc) * extra[1][rr, cols].astype(F32))
                else:
                    raise ValueError(epilogue)
                o_ref[rr, cols] = val.astype(o_ref.dtype)
        return

    assert nk == 1
    row_sub = min(ROW_SUB, tm)
    pieces = [(rs, c) for rs in range(tm // row_sub) for c in range(tn // strip)]
    if epilogue == "plain_by_residue":
        n_res = o_ref.shape[0]
        sub_per = row_sub // n_res
        for rs, c in pieces:
            rr = slice(rs * row_sub, (rs + 1) * row_sub)
            acc = product(rr, slice(c * strip, (c + 1) * strip))
            for piece in range(strip // LANES):
                lane_tile = c * (strip // LANES) + piece
                acc_ref[lane_tile, rr, :] = acc[:, piece * LANES:(piece + 1) * LANES]
                for r in range(n_res):
                    o_ref[r, rs * sub_per:(rs + 1) * sub_per,
                          lane_tile * LANES:(lane_tile + 1) * LANES] = acc_ref.at[lane_tile][
                              pl.ds(rs * row_sub + r, sub_per, stride=n_res), :].astype(o_ref.dtype)
        return

    halo = SUBLANES
    base = halo - (CONV_K - 1)

    @pl.when(i == 0)
    def _():
        acc_ref[0:halo, :] = jnp.zeros((halo, tn), F32)

    for rs, c in pieces:
        cols = slice(c * strip, (c + 1) * strip)
        r0 = rs * row_sub
        acc = product(slice(r0, r0 + row_sub), cols)
        acc_ref[halo + r0:halo + r0 + row_sub, cols] = acc
        cw = extra[0][:, cols]
        y = acc * cw[CONV_K - 1:CONV_K]
        for tap in range(CONV_K - 1):
            y = y + acc_ref[base + tap + r0:base + tap + r0 + row_sub, cols] * cw[tap:tap + 1]
        y = y * _sigmoid(y)
        if epilogue == "conv_silu_l2norm":
            segs = []
            for hh in range(strip // HEAD_DIM):
                seg = y[:, hh * HEAD_DIM:(hh + 1) * HEAD_DIM]
                segs.append(seg * lax.rsqrt(jnp.sum(seg * seg, axis=-1, keepdims=True) + EPS))
            y = jnp.concatenate(segs, axis=1)
        o_ref[r0:r0 + row_sub, cols] = y.astype(o_ref.dtype)
    acc_ref[0:halo, :] = acc_ref[tm:tm + halo, :]


def matmul_ws(a, b, *, col0=0, n=None, out_dtype=BF16, epilogue="plain", extra=(),
              tm=1024, tn=512, tk=None, b_is_nk=False, panel_col=None, residues=1,
              name="matmul_ws"):
    m, kdim = a.shape
    n_total = b.shape[0] if b_is_nk else b.shape[1]
    if n is None:
        n = n_total - col0
    if tk is None:
        tk = kdim
    assert m % tm == 0 and n % tn == 0 and kdim % tk == 0 and col0 + n <= n_total
    assert panel_col is None or b_is_nk
    nj, nk, ni = n // tn, kdim // tk, m // tm
    n_tiles = nj * nk

    def tile_jk(t):
        tile = jnp.maximum(t - 1, 0)
        return tile // nk, tile % nk

    def row_block(t, i):
        _, k = tile_jk(t)
        return jnp.where(jnp.logical_and(t > 0, k == nk - 1), i, 0)

    def a_map(t, i):
        return (jnp.where(t > 0, i, 0), tile_jk(t)[1])

    def out_map(t, i):
        return (row_block(t, i), tile_jk(t)[0])

    def staged(t, i):
        tile = jnp.minimum(t, n_tiles - 1)
        return tile // nk, tile % nk, jnp.where(t == n_tiles, ni - 1, i)

    if b_is_nk:
        assert col0 % SUBLANES == 0 and tn % (ni * SUBLANES) == 0
        cn = tn // ni

        def b_map(t, i):
            j, k, c = staged(t, i)
            first = col0 + j * tn if panel_col is None else panel_col(j)
            return (pl.multiple_of(first + c * cn, SUBLANES), pl.multiple_of(k * tk, LANES))

        b_spec = pl.BlockSpec((pl.Element(cn), pl.Element(tk)), b_map)
        bcast_shape = (tn, tk)
    else:
        assert col0 % tn == 0 and tk % (ni * SUBLANES) == 0

        def b_map(t, i):
            j, k, c = staged(t, i)
            return (k * ni + c, j + col0 // tn)

        b_spec = pl.BlockSpec((tk // ni, tn), b_map)
        bcast_shape = (tk, tn)
    in_specs = [pl.BlockSpec((tm, tk), a_map), b_spec]
    conv = epilogue.startswith("conv_silu")
    for idx, (_, ecol0) in enumerate(extra):
        assert ecol0 % tn == 0
        if conv and idx == 0:
            in_specs.append(pl.BlockSpec(
                (CONV_K, tn),
                functools.partial(lambda t, i, e0: (0, tile_jk(t)[0] + e0), e0=ecol0 // tn)))
        else:
            in_specs.append(pl.BlockSpec(
                (tm, tn),
                functools.partial(lambda t, i, e0: (row_block(t, i), tile_jk(t)[0] + e0),
                                  e0=ecol0 // tn)))
    if epilogue == "plain_by_residue":
        assert nk == 1 and tm % (residues * SUBLANES) == 0
        out_spec = pl.BlockSpec((residues, tm // residues, tn),
                                lambda t, i: (0, row_block(t, i), tile_jk(t)[0]))
        out_shape = jax.ShapeDtypeStruct((residues, m // residues, n), out_dtype)
    else:
        out_spec = pl.BlockSpec((tm, tn), out_map)
        out_shape = jax.ShapeDtypeStruct((m, n), out_dtype)
    scratch = [pltpu.VMEM(bcast_shape, BF16), pltpu.VMEM(bcast_shape, BF16)]
    if conv:
        assert nk == 1
        scratch.append(pltpu.VMEM((SUBLANES + tm, tn), F32))
    elif epilogue == "plain_by_residue":
        scratch.append(pltpu.VMEM((tn // LANES, tm, LANES), F32))
    else:
        scratch.append(pltpu.VMEM((m, tn) if nk > 1 else (SUBLANES, LANES), F32))
    return pl.pallas_call(
        functools.partial(_mm_kernel, nk=nk, tm=tm, epilogue=epilogue, n_extra=len(extra),
                          b_is_nk=b_is_nk),
        grid=(n_tiles + 1, ni),
        in_specs=in_specs,
        out_specs=out_spec,
        out_shape=out_shape,
        scratch_shapes=scratch,
        compiler_params=_params(2),
        name=name,
    )(a, b, *[arr for arr, _ in extra])


def _attn_kernel(q_ref, kp_ref, kc_ref, vp_ref, vc_ref, o_ref, lse_ref, *, heads, scale):
    n = pl.program_id(1)
    planes, sub, _ = q_ref.shape
    span = planes * sub
    rho = lax.broadcasted_iota(jnp.int32, (span, 2 * span), 0)
    kap = lax.broadcasted_iota(jnp.int32, (span, 2 * span), 1)
    in_cur = kap // span
    dist = (planes * (sub * (1 - in_cur) + rho % sub - kap % sub)
            + rho // sub - (kap % span) // sub)
    valid = (dist >= 0) & (dist <= span) & ((in_cur == 1) | (n > 0))
    lane = lax.broadcasted_iota(jnp.int32, (span, HEAD_DIM), 1)
    lse_tile = jnp.zeros((span, HEAD_DIM), F32)

    def tile(ref, cols):
        return jnp.concatenate([ref[a, :, cols] for a in range(planes)], axis=0)

    for h in range(heads):
        cols = slice(h * HEAD_DIM, (h + 1) * HEAD_DIM)
        q = tile(q_ref, cols)
        k = jnp.concatenate([tile(kp_ref, cols), tile(kc_ref, cols)], axis=0)
        v = jnp.concatenate([tile(vp_ref, cols), tile(vc_ref, cols)], axis=0)
        s = lax.dot_general(q, k, (((1,), (1,)), ((), ())), preferred_element_type=F32) * scale
        s = jnp.where(valid, s, -jnp.inf)
        m = jnp.max(s, axis=-1, keepdims=True)
        e = jnp.exp(s - m)
        l = jnp.sum(e, axis=-1, keepdims=True)
        o = (jnp.dot(e.astype(BF16), v, preferred_element_type=F32) / l).astype(o_ref.dtype)
        for a in range(planes):
            o_ref[a, :, cols] = o[a * sub:(a + 1) * sub]
        lse_tile = jnp.where(lane == h, m + jnp.log(l), lse_tile)
    for a in range(planes):
        lse_ref[a] = lse_tile[a * sub:(a + 1) * sub]


def banded_attention(q, k, v, *, heads=ATTN_HEADS, col_blocks=(0, 0, 0)):
    planes, n_sub, rows, _ = q.shape
    width = heads * HEAD_DIM
    sub = ATTN_SPAN // planes
    nb = rows // sub

    def cur_spec(cb):
        return pl.BlockSpec((planes, None, sub, width), lambda r, n: (0, r, n, cb))

    def prev_spec(cb):
        return pl.BlockSpec((planes, None, sub, width),
                            lambda r, n: (0, r, jnp.maximum(n - 1, 0), cb))

    qc, kc, vc = col_blocks
    return pl.pallas_call(
        functools.partial(_attn_kernel, heads=heads, scale=HEAD_DIM ** -0.5),
        grid=(n_sub, nb),
        in_specs=[cur_spec(qc), prev_spec(kc), cur_spec(kc), prev_spec(vc), cur_spec(vc)],
        out_specs=[pl.BlockSpec((planes, None, sub, width), lambda r, n: (0, r, n, 0)),
                   pl.BlockSpec((planes, None, sub, HEAD_DIM), lambda r, n: (0, r, n, 0))],
        out_shape=[jax.ShapeDtypeStruct((planes, n_sub, rows, width), BF16),
                   jax.ShapeDtypeStruct((planes, n_sub, rows, HEAD_DIM), F32)],
        compiler_params=_params(2),
        name="banded_attention",
    )(q, k, k, v, v)


def _attn_combine_kernel(o0_ref, o1_ref, o2_ref, l0_ref, l1_ref, l2_ref, out_ref, *, heads):
    l0, l1, l2 = l0_ref[...], l1_ref[...], l2_ref[...]
    m = jnp.maximum(jnp.maximum(l0, l1), l2)
    e0, e1, e2 = jnp.exp(l0 - m), jnp.exp(l1 - m), jnp.exp(l2 - m)
    den = e0 + e1 + e2
    w0, w1, w2 = e0 / den, e1 / den, e2 / den
    for h in range(heads):
        cols = slice(h * HEAD_DIM, (h + 1) * HEAD_DIM)
        acc = (w0[:, h:h + 1] * o0_ref[:, cols] + w1[:, h:h + 1] * o1_ref[:, cols]
               + w2[:, h:h + 1] * o2_ref[:, cols])
        out_ref[:, cols] = acc.astype(out_ref.dtype)


def attn_combine(outs, lses, *, tm=256):
    m, width = outs[0].shape
    heads = width // HEAD_DIM
    ospec = pl.BlockSpec((tm, width), lambda i: (i, 0))
    lspec = pl.BlockSpec((tm, HEAD_DIM), lambda i: (i, 0))
    return pl.pallas_call(
        functools.partial(_attn_combine_kernel, heads=heads),
        grid=(m // tm,),
        in_specs=[ospec] * 3 + [lspec] * 3,
        out_specs=ospec,
        out_shape=jax.ShapeDtypeStruct((m, width), BF16),
        compiler_params=_params(1),
        name="attn_combine",
    )(*outs, *lses)


def _softplus(x):
    return jnp.maximum(x, 0.0) + jnp.log(1.0 + jnp.exp(-jnp.abs(x)))


def _bdot(a, b, dims):
    return lax.dot_general(a, b, (dims, ((0,), (0,))), preferred_element_type=F32)


def _bmm(a, b):
    return _bdot(a, b, ((2,), (1,)))


def _split_bf16(x):
    hi = x.astype(BF16)
    lo = (x - hi.astype(F32)).astype(BF16)
    return hi, lo


def _bmm_split(a, b):
    a_hi, a_lo = _split_bf16(a)
    b_hi, b_lo = _split_bf16(b)
    return _bmm(a_hi, b_hi) + _bmm(a_hi, b_lo) + _bmm(a_lo, b_hi)


def _unit_lower_inverse_batched(a):
    c = a.shape[-1]
    b0 = SUBLANES
    ri = lax.broadcasted_iota(jnp.int32, (c, c), 0)
    ci = lax.broadcasted_iota(jnp.int32, (c, c), 1)
    below = ri > ci
    eye = (ri == ci).astype(F32)

    def same_block(b):
        return (ri // b) == (ci // b)

    def mm(x, y):
        return _bmm(x.astype(BF16), y.astype(BF16))

    d = jnp.where(same_block(b0) & below, a, 0.0)
    p = eye - d
    x = d
    for _ in range(int(math.log2(b0)) - 1):
        x = mm(x, x)
        p = p + mm(p, x)
    b = b0
    while b < c:
        off_diag = jnp.where(same_block(2 * b) & jnp.logical_not(same_block(b)) & below, a, 0.0)
        p = p - mm(mm(p, off_diag), p)
        b *= 2
    resid = eye - p - _bmm_split(a, p)
    return p + mm(p, resid)


def _dn_prep_kernel(q_ref, k_ref, v_ref, ba_ref, bat_ref, prow_ref, pcol_ref,
                    u_ref, w_ref, qg_ref, kd_ref, intra_ref, eg_ref, *, heads, chunks):
    c_len = DN_CHUNK

    ba = ba_ref[...]
    beta_col = _sigmoid(ba)
    g_col = -jnp.exp(prow_ref[0:1, :]) * _softplus(ba + prow_ref[1:2, :])
    row_in_chunk = lax.broadcasted_iota(jnp.int32, g_col.shape, 0) % c_len
    shift = 1
    while shift < c_len:
        g_col = g_col + jnp.where(row_in_chunk >= shift, pltpu.roll(g_col, shift, 0), 0.0)
        shift *= 2
    bat = bat_ref[...]
    beta_row = _sigmoid(bat)
    g_row = -jnp.exp(pcol_ref[:, 0:1])[None] * _softplus(bat + pcol_ref[:, 1:2][None])
    lane_in_chunk = lax.broadcasted_iota(jnp.int32, g_row.shape, 2)
    shift = 1
    while shift < c_len:
        g_row = g_row + jnp.where(lane_in_chunk >= shift, pltpu.roll(g_row, shift, 2), 0.0)
        shift *= 2
    eg_ref[...] = jnp.exp(g_row)

    ri = lax.broadcasted_iota(jnp.int32, (c_len, c_len), 0)
    ci = lax.broadcasted_iota(jnp.int32, (c_len, c_len), 1)
    lower = ri >= ci
    strict = ri > ci

    units = [(c, h) for c in range(chunks) for h in range(heads)]

    def rows_of(c):
        return slice(c * c_len, (c + 1) * c_len)

    def per_unit(x_ref):
        return jnp.stack([x_ref[rows_of(c), h * HEAD_DIM:(h + 1) * HEAD_DIM] for c, h in units])

    q = per_unit(q_ref).astype(F32) * HEAD_DIM ** -0.5
    k16 = per_unit(k_ref)
    k = k16.astype(F32)
    v16 = per_unit(v_ref)
    gcol = jnp.stack([g_col[rows_of(c), heads + h:heads + h + 1] for c, h in units])
    bcol = jnp.stack([beta_col[rows_of(c), h:h + 1] for c, h in units])
    grow = jnp.stack([g_row[c, heads + h:heads + h + 1, :] for c, h in units])
    brow = jnp.stack([beta_row[c, h:h + 1, :] for c, h in units])
    decay = jnp.exp(jnp.where(lower, gcol - grow, -jnp.inf))
    q16 = q.astype(BF16)
    qk = _bdot(jnp.concatenate([q16, k16], axis=1), k16, ((2,), (2,)))
    intra = qk[:, :c_len] * decay
    a = jnp.where(strict, qk[:, c_len:] * decay * bcol, 0.0)
    t_inv = _unit_lower_inverse_batched(a)
    u = _bmm((t_inv * brow).astype(BF16), v16)
    w = _bmm((t_inv * (brow * jnp.exp(grow))).astype(BF16), k16)
    g_last = grow[:, :, c_len - 1:c_len]
    qg = q * jnp.exp(gcol)
    kd = k * jnp.exp(g_last - gcol)
    for c in range(chunks):
        rs, sl = rows_of(c), slice(c * heads, (c + 1) * heads)
        u_ref[:, rs, :] = u[sl]
        w_ref[:, rs, :] = w[sl].astype(w_ref.dtype)
        qg_ref[:, rs, :] = qg[sl].astype(qg_ref.dtype)
        kd_ref[:, rs, :] = kd[sl].astype(kd_ref.dtype)
        intra_ref[:, rs, :] = intra[sl].astype(intra_ref.dtype)


def _dn_scan_kernel(egl_ref, u_ref, w_ref, qg_ref, kd_ref, intra_ref, z_ref, gn_ref, o_ref,
                    state_ref, *, heads, chunks):
    tb = pl.program_id(0)
    c_len = DN_CHUNK

    @pl.when(tb == 0)
    def _():
        state_ref[...] = jnp.zeros_like(state_ref)

    gn = gn_ref[...]
    for c in range(chunks):
        rs = slice(c * c_len, (c + 1) * c_len)
        s = state_ref[...]
        lhs = jnp.concatenate([w_ref[:, rs, :], qg_ref[:, rs, :]], axis=1)
        r = _bmm(lhs, s.astype(BF16))
        v_new = u_ref[:, rs, :] - r[:, :c_len]
        vn16 = v_new.astype(BF16)
        o = r[:, c_len:] + _bmm(intra_ref[:, rs, :], vn16)
        upd = _bdot(kd_ref[:, rs, :], vn16, ((1,), (1,)))
        base = (tb * chunks + c) * heads
        for h in range(heads):
            state_ref[h] = s[h] * egl_ref[base + h] + upd[h]
        o = o * lax.rsqrt(jnp.mean(o * o, axis=-1, keepdims=True) + EPS) * gn
        for h in range(heads):
            cs = slice(h * HEAD_DIM, (h + 1) * HEAD_DIM)
            zz = z_ref[rs, cs].astype(F32)
            o_ref[rs, cs] = (o[h] * (zz * _sigmoid(zz))).astype(o_ref.dtype)


def gated_deltanet2(qk, v, ba, a_log, dt_bias, z, dn_norm, *, prep_chunks=4, scan_chunks=4):
    s_len, width = z.shape
    heads = width // HEAD_DIM
    c_len = DN_CHUNK
    n_chunks = s_len // c_len
    bat = ba[:, :2 * heads].reshape(n_chunks, c_len, 2 * heads).transpose(0, 2, 1)
    prow = jnp.zeros((2, HEAD_DIM), F32)
    prow = prow.at[0, heads:2 * heads].set(a_log.astype(F32))
    prow = prow.at[1, heads:2 * heads].set(dt_bias.astype(F32))
    pcol = jnp.zeros((2 * heads, 2), F32)
    pcol = pcol.at[heads:, 0].set(a_log.astype(F32)).at[heads:, 1].set(dt_bias.astype(F32))

    rows = prep_chunks * c_len

    def main_spec(i):
        return pl.BlockSpec((rows, width), lambda t: (t, i))

    def prep_out(last):
        return pl.BlockSpec((heads, rows, last), lambda t: (0, t, 0))

    u, w, qg, kd, intra, eg = pl.pallas_call(
        functools.partial(_dn_prep_kernel, heads=heads, chunks=prep_chunks),
        grid=(s_len // rows,),
        in_specs=[main_spec(0), main_spec(1), main_spec(0),
                  pl.BlockSpec((rows, HEAD_DIM), lambda t: (t, 0)),
                  pl.BlockSpec((prep_chunks, 2 * heads, c_len), lambda t: (t, 0, 0)),
                  pl.BlockSpec((2, HEAD_DIM), lambda t: (0, 0)),
                  pl.BlockSpec((2 * heads, 2), lambda t: (0, 0))],
        out_specs=[prep_out(HEAD_DIM), prep_out(HEAD_DIM), prep_out(HEAD_DIM),
                   prep_out(HEAD_DIM), prep_out(c_len),
                   pl.BlockSpec((prep_chunks, 2 * heads, c_len), lambda t: (t, 0, 0))],
        out_shape=[jax.ShapeDtypeStruct((heads, s_len, HEAD_DIM), F32),
                   jax.ShapeDtypeStruct((heads, s_len, HEAD_DIM), BF16),
                   jax.ShapeDtypeStruct((heads, s_len, HEAD_DIM), BF16),
                   jax.ShapeDtypeStruct((heads, s_len, HEAD_DIM), BF16),
                   jax.ShapeDtypeStruct((heads, s_len, c_len), BF16),
                   jax.ShapeDtypeStruct((n_chunks, 2 * heads, c_len), F32)],
        compiler_params=_params(1),
        name="deltanet_prep",
    )(qk, qk, v, ba, bat, prow, pcol)

    egl = eg[:, heads:, c_len - 1].reshape(n_chunks * heads)

    scan_rows = scan_chunks * c_len

    def head_major(last):
        return pl.BlockSpec((heads, scan_rows, last), lambda t: (0, t, 0))

    nat = pl.BlockSpec((scan_rows, width), lambda t: (t, 0))
    return pl.pallas_call(
        functools.partial(_dn_scan_kernel, heads=heads, chunks=scan_chunks),
        grid=(s_len // scan_rows,),
        in_specs=[pl.BlockSpec(memory_space=pltpu.SMEM),
                  head_major(HEAD_DIM), head_major(HEAD_DIM), head_major(HEAD_DIM),
                  head_major(HEAD_DIM), head_major(c_len), nat,
                  pl.BlockSpec((1, HEAD_DIM), lambda t: (0, 0))],
        out_specs=nat,
        out_shape=jax.ShapeDtypeStruct((s_len, width), BF16),
        scratch_shapes=[pltpu.VMEM((heads, HEAD_DIM, HEAD_DIM), F32)],
        compiler_params=_params(1),
        name="deltanet_scan",
    )(egl, u, w, qg, kd, intra, z, dn_norm.reshape(1, HEAD_DIM).astype(F32))


def _proj_norm_kernel(p_ref, w_ref, g_ref, o_ref, wcast_ref):
    @pl.when(pl.program_id(0) == 0)
    def _():
        wcast_ref[...] = w_ref[...].astype(BF16)

    y = jnp.dot(p_ref[...].astype(BF16), wcast_ref[...], preferred_element_type=F32)
    ms = jnp.mean(y * y, axis=-1, keepdims=True)
    o_ref[...] = (y * lax.rsqrt(ms + EPS) * g_ref[...]).astype(o_ref.dtype)


def proj_norm(p, w, g, out_dtype, *, tm=512):
    m, kdim = p.shape
    n = w.shape[1]
    return pl.pallas_call(
        _proj_norm_kernel,
        grid=(m // tm,),
        in_specs=[pl.BlockSpec((tm, kdim), lambda i: (i, 0)),
                  pl.BlockSpec((kdim, n), lambda i: (0, 0)),
                  pl.BlockSpec((1, n), lambda i: (0, 0))],
        out_specs=pl.BlockSpec((tm, n), lambda i: (i, 0)),
        out_shape=jax.ShapeDtypeStruct((m, n), out_dtype),
        scratch_shapes=[pltpu.VMEM((kdim, n), BF16)],
        compiler_params=_params(1),
        name="proj_norm",
    )(p, w, g.reshape(1, n).astype(F32))


def _layer(x, p, w_in, conv_w, a_log, dt_bias, dn_norm, w_attn_up, w_dn_up, w_out, w_mlp_up,
           w_mlp_down, w_ple_gate, w_ple_proj, norm_mix, norm_mlp, norm_ple, ple_post_norm):
    s_len, d_model = x.shape
    attn_w = ATTN_HEADS * HEAD_DIM
    dn_w = DN_HEADS * HEAD_DIM
    n_groups = len(ATTN_DILATIONS)
    c_qkv_b = 3 * n_groups * attn_w
    c_z = c_qkv_b + 3 * dn_w
    c_ba = c_z + dn_w
    c_gate = c_ba + 2 * DN_HEADS
    mm = matmul_ws

    w_in_t = w_in.T
    h = rmsnorm(x, norm_mix, BF16)
    max_dil = max(ATTN_DILATIONS)
    qkv_g0 = mm(h, w_in_t, b_is_nk=True, n=3 * attn_w, tn=attn_w, out_dtype=BF16,
                panel_col=lambda j: j * (n_groups * attn_w), name="proj_attn_g0")
    qkv_g12 = mm(h, w_in_t, b_is_nk=True, n=6 * attn_w, tn=attn_w, out_dtype=BF16,
                 epilogue="plain_by_residue", residues=max_dil,
                 panel_col=lambda j: (j // 2) * (n_groups * attn_w) + (1 + j % 2) * attn_w,
                 name="proj_attn_g12")
    qk_b = mm(h, w_in_t, b_is_nk=True, col0=c_qkv_b, n=2 * dn_w, tn=1024, out_dtype=BF16,
              epilogue="conv_silu_l2norm", extra=((conv_w, 0),), name="proj_dn_qk")
    v_b = mm(h, w_in_t, b_is_nk=True, col0=c_qkv_b + 2 * dn_w, n=dn_w, tn=1024, out_dtype=BF16,
             epilogue="conv_silu", extra=((conv_w, 2 * dn_w),), name="proj_dn_v")
    z_b = mm(h, w_in_t, b_is_nk=True, col0=c_z, n=dn_w, tn=1024, out_dtype=BF16, name="proj_z")
    ba = mm(h, w_in_t, b_is_nk=True, col0=c_ba, n=LANES, tn=LANES, out_dtype=F32, name="proj_ba")
    gates = mm(h, w_in_t, b_is_nk=True, col0=c_gate, n=2 * d_model, tn=1024, out_dtype=BF16,
               epilogue="sigmoid", name="proj_gates")

    g0 = qkv_g0.reshape(1, 1, s_len, 3 * attn_w)
    o0, lse0 = banded_attention(g0, g0, g0, col_blocks=(0, 1, 2))
    per = s_len // max_dil
    g1 = qkv_g12.reshape(4, 4, per, 6 * attn_w)
    o1, lse1 = banded_attention(g1, g1, g1, col_blocks=(0, 2, 4))
    g2 = qkv_g12.reshape(1, max_dil, per, 6 * attn_w)
    o2, lse2 = banded_attention(g2, g2, g2, col_blocks=(1, 3, 5))

    def natural(t):
        return t.transpose(2, 0, 1, 3).reshape(s_len, t.shape[-1])

    o_a = attn_combine([o0.reshape(s_len, attn_w), natural(o1), natural(o2)],
                       [lse0.reshape(s_len, HEAD_DIM), natural(lse1), natural(lse2)])

    o_b = gated_deltanet2(qk_b, v_b, ba, a_log, dt_bias, z_b, dn_norm)

    t_a = mm(o_a, w_attn_up, tn=1024, out_dtype=BF16, epilogue="mul", extra=((gates, 0),),
             name="attn_up")
    merged = mm(o_b, w_dn_up, tn=1024, out_dtype=BF16, epilogue="mul_add",
                extra=((gates, d_model), (t_a, 0)), name="dn_up")
    x = mm(merged, w_out, tm=512, tn=1024, out_dtype=F32, epilogue="residual", extra=((x, 0),),
           name="out_proj")

    h = rmsnorm(x, norm_mlp, BF16)
    u = mm(h, w_mlp_up, tn=1024, out_dtype=BF16, epilogue="relu2", name="mlp_up")
    x = mm(u, w_mlp_down, out_dtype=F32, epilogue="residual", extra=((x, 0),), tk=4096,
           name="mlp_down")

    h = rmsnorm(x, norm_ple, BF16)
    pp = proj_norm(p, w_ple_proj, ple_post_norm, F32)
    x = mm(h, w_ple_gate, tm=512, tn=1024, out_dtype=F32, epilogue="ple",
           extra=((x, 0), (pp, 0)), name="ple_gate")
    return x


def kernel(x, p, w_in, conv_w, dn_a_log, dn_dt_bias, dn_norm, w_attn_up, w_dn_up, w_out,
           w_mlp_up, w_mlp_down, w_ple_gate, w_ple_proj, norm_mix, norm_mlp, norm_ple,
           ple_post_norm, final_norm):
    b, s_len, d_model = x.shape
    assert b == 1
    depth = w_in.shape[0]
    xs = x.reshape(s_len, d_model)
    for i in range(depth):
        xs = _layer(xs, p[i, 0], w_in[i], conv_w[i], dn_a_log[i], dn_dt_bias[i], dn_norm[i],
                    w_attn_up[i], w_dn_up[i], w_out[i], w_mlp_up[i], w_mlp_down[i],
                    w_ple_gate[i], w_ple_proj[i], norm_mix[i], norm_mlp[i], norm_ple[i],
                    ple_post_norm[i])
    out = rmsnorm(xs, final_norm, x.dtype)
    return out.reshape(b, s_len, d_model)
```

```python
import functools
import math

import jax
import jax.numpy as jnp
from jax import lax
from jax.experimental import pallas as pl
from jax.experimental.pallas import tpu as pltpu

F32 = jnp.float32
BF16 = jnp.bfloat16

EPS = 1e-6
LANES = 128
SUBLANES = 8
MXU_WIDTH = 256
ROW_SUB = 256
HEAD_DIM = 128
ATTN_SPAN = 128
ATTN_DILATIONS = (1, 4, 16)
ATTN_HEADS = 8
DN_HEADS = 16
DN_CHUNK = 64
CONV_K = 4

VMEM_LIMIT_BYTES = 56 * 1024 * 1024


def _params(n_axes):
    return pltpu.CompilerParams(
        dimension_semantics=("arbitrary",) * n_axes,
        vmem_limit_bytes=VMEM_LIMIT_BYTES,
    )


def _rmsnorm_kernel(x_ref, g_ref, o_ref):
    x = x_ref[...].astype(F32)
    ms = jnp.mean(x * x, axis=-1, keepdims=True)
    o_ref[...] = (x * lax.rsqrt(ms + EPS) * g_ref[...]).astype(o_ref.dtype)


def rmsnorm(x, g, out_dtype, *, tm=512):
    m, d = x.shape
    return pl.pallas_call(
        _rmsnorm_kernel,
        grid=(m // tm,),
        in_specs=[pl.BlockSpec((tm, d), lambda i: (i, 0)),
                  pl.BlockSpec((1, d), lambda i: (0, 0))],
        out_specs=pl.BlockSpec((tm, d), lambda i: (i, 0)),
        out_shape=jax.ShapeDtypeStruct((m, d), out_dtype),
        compiler_params=_params(1),
        name="rmsnorm",
    )(x, g.reshape(1, d).astype(F32))


def _sigmoid(x):
    return 1.0 / (1.0 + jnp.exp(-x))


def _mm_kernel(*refs, nk, tm, epilogue, n_extra, b_is_nk):
    a_ref, bchunk_ref = refs[0], refs[1]
    extra = refs[2:2 + n_extra]
    o_ref = refs[2 + n_extra]
    w_even, w_odd = refs[3 + n_extra], refs[4 + n_extra]
    acc_ref = refs[5 + n_extra]
    t = pl.program_id(0)
    i = pl.program_id(1)
    chunk_rows = bchunk_ref.shape[0]

    def stage(w_ref):
        rows = pl.ds(pl.multiple_of(i * chunk_rows, chunk_rows), chunk_rows)
        w_ref[rows, :] = bchunk_ref[...].astype(BF16)

    def compute(w_ref):
        _mm_compute(a_ref, w_ref, extra, o_ref, acc_ref, t - 1, i, nk=nk, tm=tm,
                    epilogue=epilogue, b_is_nk=b_is_nk)

    @pl.when(t == 0)
    def _():
        stage(w_even)

    @pl.when(jnp.logical_and(t > 0, t % 2 == 1))
    def _():
        stage(w_odd)
        compute(w_even)

    @pl.when(jnp.logical_and(t > 0, t % 2 == 0))
    def _():
        stage(w_even)
        compute(w_odd)


def _mm_compute(a_ref, w_ref, extra, o_ref, acc_ref, tile, i, *, nk, tm, epilogue, b_is_nk):
    k = tile % nk
    tn = o_ref.shape[-1]
    strip = min(MXU_WIDTH, tn)
    conv = epilogue.startswith("conv_silu")

    def product(row_slice, cols):
        if b_is_nk:
            return lax.dot_general(a_ref[row_slice, :], w_ref[cols, :], (((1,), (1,)), ((), ())),
                                   preferred_element_type=F32)
        return jnp.dot(a_ref[row_slice, :], w_ref[:, cols], preferred_element_type=F32)

    if not conv and epilogue != "plain_by_residue":
        row_sub = min(ROW_SUB, tm)
        for rs in range(tm // row_sub):
            rr = slice(rs * row_sub, (rs + 1) * row_sub)
            for c in range(tn // strip):
                cols = slice(c * strip, (c + 1) * strip)
                acc = product(rr, cols)
                if nk > 1:
                    arows = pl.ds(pl.multiple_of(i * tm + rs * row_sub, row_sub), row_sub)
                    acc = acc + jnp.where(k == 0, 0.0, acc_ref[arows, cols])
                    acc_ref[arows, cols] = acc
                if epilogue == "plain":
                    val = acc
                elif epilogue == "relu2":
                    r = jnp.maximum(acc, 0.0)
                    val = r * r
                elif epilogue == "sigmoid":
                    val = _sigmoid(acc)
                elif epilogue == "residual":
                    val = extra[0][rr, cols].astype(F32) + acc
                elif epilogue == "mul":
                    val = extra[0][rr, cols].astype(F32) * acc
                elif epilogue == "mul_add":
                    val = extra[1][rr, cols].astype(F32) + extra[0][rr, cols].astype(F32) * acc
                elif epilogue == "ple":
                    val = (extra[0][rr, cols].astype(F32)
                           + _sigmoid(acc) * extra[1][rr, cols].astype(F32))
                else:
                    raise ValueError(epilogue)
                o_ref[rr, cols] = val.astype(o_ref.dtype)
        return

    assert nk == 1
    all_rows = slice(0, tm)
    if epilogue == "plain_by_residue":
        for c in range(tn // strip):
            acc = product(all_rows, slice(c * strip, (c + 1) * strip))
            for piece in range(strip // LANES):
                acc_ref[c * (strip // LANES) + piece] = acc[:, piece * LANES:(piece + 1) * LANES]
        n_res = o_ref.shape[0]
        for piece in range(tn // LANES):
            for r in range(n_res):
                o_ref[r, :, piece * LANES:(piece + 1) * LANES] = acc_ref.at[piece][
                    pl.ds(r, tm // n_res, stride=n_res), :].astype(o_ref.dtype)
        return

    halo = SUBLANES

    @pl.when(i == 0)
    def _():
        acc_ref[0:halo, :] = jnp.zeros((halo, tn), F32)

    for c in range(tn // strip):
        cols = slice(c * strip, (c + 1) * strip)
        acc_ref[halo:halo + tm, cols] = product(all_rows, cols)
    for c in range(tn // strip):
        cols = slice(c * strip, (c + 1) * strip)
        acc = acc_ref[halo:halo + tm, cols]
        cw = extra[0][:, cols]
        base = halo - (CONV_K - 1)
        y = acc * cw[CONV_K - 1:CONV_K]
        for tap in range(CONV_K - 1):
            y = y + acc_ref[base + tap:base + tap + tm, cols] * cw[tap:tap + 1]
        acc_ref[0:halo, cols] = acc[tm - halo:, :]
        y = y * _sigmoid(y)
        if epilogue == "conv_silu_l2norm":
            segs = []
            for hh in range(strip // HEAD_DIM):
                seg = y[:, hh * HEAD_DIM:(hh + 1) * HEAD_DIM]
                segs.append(seg * lax.rsqrt(jnp.sum(seg * seg, axis=-1, keepdims=True) + EPS))
            y = jnp.concatenate(segs, axis=1)
        o_ref[:, cols] = y.astype(o_ref.dtype)


def matmul_ws(a, b, *, col0=0, n=None, out_dtype=BF16, epilogue="plain", extra=(),
              tm=1024, tn=512, tk=None, b_is_nk=False, panel_col=None, residues=1,
              name="matmul_ws"):
    m, kdim = a.shape
    n_total = b.shape[0] if b_is_nk else b.shape[1]
    if n is None:
        n = n_total - col0
    if tk is None:
        tk = kdim
    assert m % tm == 0 and n % tn == 0 and kdim % tk == 0 and col0 + n <= n_total
    assert panel_col is None or b_is_nk
    nj, nk, ni = n // tn, kdim // tk, m // tm
    n_tiles = nj * nk

    def tile_jk(t):
        tile = jnp.maximum(t - 1, 0)
        return tile // nk, tile % nk

    def row_block(t, i):
        _, k = tile_jk(t)
        return jnp.where(jnp.logical_and(t > 0, k == nk - 1), i, 0)

    def a_map(t, i):
        return (jnp.where(t > 0, i, 0), tile_jk(t)[1])

    def out_map(t, i):
        return (row_block(t, i), tile_jk(t)[0])

    def staged(t, i):
        tile = jnp.minimum(t, n_tiles - 1)
        return tile // nk, tile % nk, jnp.where(t == n_tiles, ni - 1, i)

    if b_is_nk:
        assert col0 % SUBLANES == 0 and tn % (ni * SUBLANES) == 0
        cn = tn // ni

        def b_map(t, i):
            j, k, c = staged(t, i)
            first = col0 + j * tn if panel_col is None else panel_col(j)
            return (pl.multiple_of(first + c * cn, SUBLANES), pl.multiple_of(k * tk, LANES))

        b_spec = pl.BlockSpec((pl.Element(cn), pl.Element(tk)), b_map)
        bcast_shape = (tn, tk)
    else:
        assert col0 % tn == 0 and tk % (ni * SUBLANES) == 0

        def b_map(t, i):
            j, k, c = staged(t, i)
            return (k * ni + c, j + col0 // tn)

        b_spec = pl.BlockSpec((tk // ni, tn), b_map)
        bcast_shape = (tk, tn)
    in_specs = [pl.BlockSpec((tm, tk), a_map), b_spec]
    conv = epilogue.startswith("conv_silu")
    for idx, (_, ecol0) in enumerate(extra):
        assert ecol0 % tn == 0
        if conv and idx == 0:
            in_specs.append(pl.BlockSpec(
                (CONV_K, tn),
                functools.partial(lambda t, i, e0: (0, tile_jk(t)[0] + e0), e0=ecol0 // tn)))
        else:
            in_specs.append(pl.BlockSpec(
                (tm, tn),
                functools.partial(lambda t, i, e0: (row_block(t, i), tile_jk(t)[0] + e0),
                                  e0=ecol0 // tn)))
    if epilogue == "plain_by_residue":
        assert nk == 1 and tm % (residues * SUBLANES) == 0
        out_spec = pl.BlockSpec((residues, tm // residues, tn),
                                lambda t, i: (0, row_block(t, i), tile_jk(t)[0]))
        out_shape = jax.ShapeDtypeStruct((residues, m // residues, n), out_dtype)
    else:
        out_spec = pl.BlockSpec((tm, tn), out_map)
        out_shape = jax.ShapeDtypeStruct((m, n), out_dtype)
    scratch = [pltpu.VMEM(bcast_shape, BF16), pltpu.VMEM(bcast_shape, BF16)]
    if conv:
        assert nk == 1
        scratch.append(pltpu.VMEM((SUBLANES + tm, tn), F32))
    elif epilogue == "plain_by_residue":
        scratch.append(pltpu.VMEM((tn // LANES, tm, LANES), F32))
    else:
        scratch.append(pltpu.VMEM((m, tn) if nk > 1 else (SUBLANES, LANES), F32))
    return pl.pallas_call(
        functools.partial(_mm_kernel, nk=nk, tm=tm, epilogue=epilogue, n_extra=len(extra),
                          b_is_nk=b_is_nk),
        grid=(n_tiles + 1, ni),
        in_specs=in_specs,
        out_specs=out_spec,
        out_shape=out_shape,
        scratch_shapes=scratch,
        compiler_params=_params(2),
        name=name,
    )(a, b, *[arr for arr, _ in extra])


def _attn_kernel(q_ref, kp_ref, kc_ref, vp_ref, vc_ref, o_ref, lse_ref, *, heads, scale):
    n = pl.program_id(1)
    planes, sub, _ = q_ref.shape
    span = planes * sub
    rho = lax.broadcasted_iota(jnp.int32, (span, 2 * span), 0)
    kap = lax.broadcasted_iota(jnp.int32, (span, 2 * span), 1)
    in_cur = kap // span
    dist = (planes * (sub * (1 - in_cur) + rho % sub - kap % sub)
            + rho // sub - (kap % span) // sub)
    valid = (dist >= 0) & (dist <= span) & ((in_cur == 1) | (n > 0))
    lane = lax.broadcasted_iota(jnp.int32, (span, HEAD_DIM), 1)
    lse_tile = jnp.zeros((span, HEAD_DIM), F32)

    def tile(ref, cols):
        return jnp.concatenate([ref[a, :, cols] for a in range(planes)], axis=0)

    for h in range(heads):
        cols = slice(h * HEAD_DIM, (h + 1) * HEAD_DIM)
        q = tile(q_ref, cols)
        k = jnp.concatenate([tile(kp_ref, cols), tile(kc_ref, cols)], axis=0)
        v = jnp.concatenate([tile(vp_ref, cols), tile(vc_ref, cols)], axis=0)
        s = lax.dot_general(q, k, (((1,), (1,)), ((), ())), preferred_element_type=F32) * scale
        s = jnp.where(valid, s, -jnp.inf)
        m = jnp.max(s, axis=-1, keepdims=True)
        e = jnp.exp(s - m)
        l = jnp.sum(e, axis=-1, keepdims=True)
        o = (jnp.dot(e.astype(BF16), v, preferred_element_type=F32) / l).astype(o_ref.dtype)
        for a in range(planes):
            o_ref[a, :, cols] = o[a * sub:(a + 1) * sub]
        lse_tile = jnp.where(lane == h, m + jnp.log(l), lse_tile)
    for a in range(planes):
        lse_ref[a] = lse_tile[a * sub:(a + 1) * sub]


def banded_attention(q, k, v, *, heads=ATTN_HEADS, col_blocks=(0, 0, 0)):
    planes, n_sub, rows, _ = q.shape
    width = heads * HEAD_DIM
    sub = ATTN_SPAN // planes
    nb = rows // sub

    def cur_spec(cb):
        return pl.BlockSpec((planes, None, sub, width), lambda r, n: (0, r, n, cb))

    def prev_spec(cb):
        return pl.BlockSpec((planes, None, sub, width),
                            lambda r, n: (0, r, jnp.maximum(n - 1, 0), cb))

    qc, kc, vc = col_blocks
    return pl.pallas_call(
        functools.partial(_attn_kernel, heads=heads, scale=HEAD_DIM ** -0.5),
        grid=(n_sub, nb),
        in_specs=[cur_spec(qc), prev_spec(kc), cur_spec(kc), prev_spec(vc), cur_spec(vc)],
        out_specs=[pl.BlockSpec((planes, None, sub, width), lambda r, n: (0, r, n, 0)),
                   pl.BlockSpec((planes, None, sub, HEAD_DIM), lambda r, n: (0, r, n, 0))],
        out_shape=[jax.ShapeDtypeStruct((planes, n_sub, rows, width), BF16),
                   jax.ShapeDtypeStruct((planes, n_sub, rows, HEAD_DIM), F32)],
        compiler_params=_params(2),
        name="banded_attention",
    )(q, k, k, v, v)


def _attn_combine_kernel(o0_ref, o1_ref, o2_ref, l0_ref, l1_ref, l2_ref, out_ref, *, heads):
    l0, l1, l2 = l0_ref[...], l1_ref[...], l2_ref[...]
    m = jnp.maximum(jnp.maximum(l0, l1), l2)
    e0, e1, e2 = jnp.exp(l0 - m), jnp.exp(l1 - m), jnp.exp(l2 - m)
    den = e0 + e1 + e2
    w0, w1, w2 = e0 / den, e1 / den, e2 / den
    for h in range(heads):
        cols = slice(h * HEAD_DIM, (h + 1) * HEAD_DIM)
        acc = (w0[:, h:h + 1] * o0_ref[:, cols] + w1[:, h:h + 1] * o1_ref[:, cols]
               + w2[:, h:h + 1] * o2_ref[:, cols])
        out_ref[:, cols] = acc.astype(out_ref.dtype)


def attn_combine(outs, lses, *, tm=256):
    m, width = outs[0].shape
    heads = width // HEAD_DIM
    ospec = pl.BlockSpec((tm, width), lambda i: (i, 0))
    lspec = pl.BlockSpec((tm, HEAD_DIM), lambda i: (i, 0))
    return pl.pallas_call(
        functools.partial(_attn_combine_kernel, heads=heads),
        grid=(m // tm,),
        in_specs=[ospec] * 3 + [lspec] * 3,
        out_specs=ospec,
        out_shape=jax.ShapeDtypeStruct((m, width), BF16),
        compiler_params=_params(1),
        name="attn_combine",
    )(*outs, *lses)


def _softplus(x):
    return jnp.maximum(x, 0.0) + jnp.log(1.0 + jnp.exp(-jnp.abs(x)))


def _bdot(a, b, dims):
    return lax.dot_general(a, b, (dims, ((0,), (0,))), preferred_element_type=F32)


def _bmm(a, b):
    return _bdot(a, b, ((2,), (1,)))


def _split_bf16(x):
    hi = x.astype(BF16)
    lo = (x - hi.astype(F32)).astype(BF16)
    return hi, lo


def _bmm_split(a, b):
    a_hi, a_lo = _split_bf16(a)
    b_hi, b_lo = _split_bf16(b)
    return _bmm(a_hi, b_hi) + _bmm(a_hi, b_lo) + _bmm(a_lo, b_hi)


def _unit_lower_inverse_batched(a):
    c = a.shape[-1]
    b0 = SUBLANES
    ri = lax.broadcasted_iota(jnp.int32, (c, c), 0)
    ci = lax.broadcasted_iota(jnp.int32, (c, c), 1)
    below = ri > ci
    eye = (ri == ci).astype(F32)

    def same_block(b):
        return (ri // b) == (ci // b)

    def mm(x, y):
        return _bmm(x.astype(BF16), y.astype(BF16))

    d = jnp.where(same_block(b0) & below, a, 0.0)
    p = eye - d
    x = d
    for _ in range(int(math.log2(b0)) - 1):
        x = mm(x, x)
        p = p + mm(p, x)
    b = b0
    while b < c:
        off_diag = jnp.where(same_block(2 * b) & jnp.logical_not(same_block(b)) & below, a, 0.0)
        p = p - mm(mm(p, off_diag), p)
        b *= 2
    resid = eye - p - _bmm_split(a, p)
    return p + mm(p, resid)


def _dn_prep_kernel(q_ref, k_ref, v_ref, ba_ref, bat_ref, prow_ref, pcol_ref,
                    u_ref, w_ref, qg_ref, kd_ref, intra_ref, eg_ref, *, heads, chunks):
    c_len = DN_CHUNK

    ba = ba_ref[...]
    beta_col = _sigmoid(ba)
    g_col = -jnp.exp(prow_ref[0:1, :]) * _softplus(ba + prow_ref[1:2, :])
    row_in_chunk = lax.broadcasted_iota(jnp.int32, g_col.shape, 0) % c_len
    shift = 1
    while shift < c_len:
        g_col = g_col + jnp.where(row_in_chunk >= shift, pltpu.roll(g_col, shift, 0), 0.0)
        shift *= 2
    bat = bat_ref[...]
    beta_row = _sigmoid(bat)
    g_row = -jnp.exp(pcol_ref[:, 0:1])[None] * _softplus(bat + pcol_ref[:, 1:2][None])
    lane_in_chunk = lax.broadcasted_iota(jnp.int32, g_row.shape, 2)
    shift = 1
    while shift < c_len:
        g_row = g_row + jnp.where(lane_in_chunk >= shift, pltpu.roll(g_row, shift, 2), 0.0)
        shift *= 2
    eg_ref[...] = jnp.exp(g_row)

    ri = lax.broadcasted_iota(jnp.int32, (c_len, c_len), 0)
    ci = lax.broadcasted_iota(jnp.int32, (c_len, c_len), 1)
    lower = ri >= ci
    strict = ri > ci

    units = [(c, h) for c in range(chunks) for h in range(heads)]

    def rows_of(c):
        return slice(c * c_len, (c + 1) * c_len)

    def per_unit(x_ref):
        return jnp.stack([x_ref[rows_of(c), h * HEAD_DIM:(h + 1) * HEAD_DIM] for c, h in units])

    q = per_unit(q_ref).astype(F32) * HEAD_DIM ** -0.5
    k16 = per_unit(k_ref)
    k = k16.astype(F32)
    v16 = per_unit(v_ref)
    gcol = jnp.stack([g_col[rows_of(c), heads + h:heads + h + 1] for c, h in units])
    bcol = jnp.stack([beta_col[rows_of(c), h:h + 1] for c, h in units])
    grow = jnp.stack([g_row[c, heads + h:heads + h + 1, :] for c, h in units])
    brow = jnp.stack([beta_row[c, h:h + 1, :] for c, h in units])
    decay = jnp.exp(jnp.where(lower, gcol - grow, -jnp.inf))
    q16 = q.astype(BF16)
    qk = _bdot(jnp.concatenate([q16, k16], axis=1), k16, ((2,), (2,)))
    intra = qk[:, :c_len] * decay
    a = jnp.where(strict, qk[:, c_len:] * decay * bcol, 0.0)
    t_inv = _unit_lower_inverse_batched(a)
    u = _bmm((t_inv * brow).astype(BF16), v16)
    w = _bmm((t_inv * (brow * jnp.exp(grow))).astype(BF16), k16)
    g_last = grow[:, :, c_len - 1:c_len]
    qg = q * jnp.exp(gcol)
    kd = k * jnp.exp(g_last - gcol)
    for c in range(chunks):
        rs, sl = rows_of(c), slice(c * heads, (c + 1) * heads)
        u_ref[:, rs, :] = u[sl]
        w_ref[:, rs, :] = w[sl].astype(w_ref.dtype)
        qg_ref[:, rs, :] = qg[sl].astype(qg_ref.dtype)
        kd_ref[:, rs, :] = kd[sl].astype(kd_ref.dtype)
        intra_ref[:, rs, :] = intra[sl].astype(intra_ref.dtype)


def _dn_scan_kernel(egl_ref, u_ref, w_ref, qg_ref, kd_ref, intra_ref, z_ref, gn_ref, o_ref,
                    state_ref, *, heads, chunks):
    tb = pl.program_id(0)
    c_len = DN_CHUNK

    @pl.when(tb == 0)
    def _():
        state_ref[...] = jnp.zeros_like(state_ref)

    gn = gn_ref[...]
    for c in range(chunks):
        rs = slice(c * c_len, (c + 1) * c_len)
        s = state_ref[...]
        lhs = jnp.concatenate([w_ref[:, rs, :], qg_ref[:, rs, :]], axis=1)
        r = _bmm(lhs, s.astype(BF16))
        v_new = u_ref[:, rs, :] - r[:, :c_len]
        vn16 = v_new.astype(BF16)
        o = r[:, c_len:] + _bmm(intra_ref[:, rs, :], vn16)
        upd = _bdot(kd_ref[:, rs, :], vn16, ((1,), (1,)))
        base = (tb * chunks + c) * heads
        for h in range(heads):
            state_ref[h] = s[h] * egl_ref[base + h] + upd[h]
        o = o * lax.rsqrt(jnp.mean(o * o, axis=-1, keepdims=True) + EPS) * gn
        for h in range(heads):
            cs = slice(h * HEAD_DIM, (h + 1) * HEAD_DIM)
            zz = z_ref[rs, cs].astype(F32)
            o_ref[rs, cs] = (o[h] * (zz * _sigmoid(zz))).astype(o_ref.dtype)


def gated_deltanet2(qk, v, ba, a_log, dt_bias, z, dn_norm, *, prep_chunks=4, scan_chunks=4):
    s_len, width = z.shape
    heads = width // HEAD_DIM
    c_len = DN_CHUNK
    n_chunks = s_len // c_len
    bat = ba[:, :2 * heads].reshape(n_chunks, c_len, 2 * heads).transpose(0, 2, 1)
    prow = jnp.zeros((2, HEAD_DIM), F32)
    prow = prow.at[0, heads:2 * heads].set(a_log.astype(F32))
    prow = prow.at[1, heads:2 * heads].set(dt_bias.astype(F32))
    pcol = jnp.zeros((2 * heads, 2), F32)
    pcol = pcol.at[heads:, 0].set(a_log.astype(F32)).at[heads:, 1].set(dt_bias.astype(F32))

    rows = prep_chunks * c_len

    def main_spec(i):
        return pl.BlockSpec((rows, width), lambda t: (t, i))

    def prep_out(last):
        return pl.BlockSpec((heads, rows, last), lambda t: (0, t, 0))

    u, w, qg, kd, intra, eg = pl.pallas_call(
        functools.partial(_dn_prep_kernel, heads=heads, chunks=prep_chunks),
        grid=(s_len // rows,),
        in_specs=[main_spec(0), main_spec(1), main_spec(0),
                  pl.BlockSpec((rows, HEAD_DIM), lambda t: (t, 0)),
                  pl.BlockSpec((prep_chunks, 2 * heads, c_len), lambda t: (t, 0, 0)),
                  pl.BlockSpec((2, HEAD_DIM), lambda t: (0, 0)),
                  pl.BlockSpec((2 * heads, 2), lambda t: (0, 0))],
        out_specs=[prep_out(HEAD_DIM), prep_out(HEAD_DIM), prep_out(HEAD_DIM),
                   prep_out(HEAD_DIM), prep_out(c_len),
                   pl.BlockSpec((prep_chunks, 2 * heads, c_len), lambda t: (t, 0, 0))],
        out_shape=[jax.ShapeDtypeStruct((heads, s_len, HEAD_DIM), F32),
                   jax.ShapeDtypeStruct((heads, s_len, HEAD_DIM), BF16),
                   jax.ShapeDtypeStruct((heads, s_len, HEAD_DIM), BF16),
                   jax.ShapeDtypeStruct((heads, s_len, HEAD_DIM), BF16),
                   jax.ShapeDtypeStruct((heads, s_len, c_len), BF16),
                   jax.ShapeDtypeStruct((n_chunks, 2 * heads, c_len), F32)],
        compiler_params=_params(1),
        name="deltanet_prep",
    )(qk, qk, v, ba, bat, prow, pcol)

    egl = eg[:, heads:, c_len - 1].reshape(n_chunks * heads)

    scan_rows = scan_chunks * c_len

    def head_major(last):
        return pl.BlockSpec((heads, scan_rows, last), lambda t: (0, t, 0))

    nat = pl.BlockSpec((scan_rows, width), lambda t: (t, 0))
    return pl.pallas_call(
        functools.partial(_dn_scan_kernel, heads=heads, chunks=scan_chunks),
        grid=(s_len // scan_rows,),
        in_specs=[pl.BlockSpec(memory_space=pltpu.SMEM),
                  head_major(HEAD_DIM), head_major(HEAD_DIM), head_major(HEAD_DIM),
                  head_major(HEAD_DIM), head_major(c_len), nat,
                  pl.BlockSpec((1, HEAD_DIM), lambda t: (0, 0))],
        out_specs=nat,
        out_shape=jax.ShapeDtypeStruct((s_len, width), BF16),
        scratch_shapes=[pltpu.VMEM((heads, HEAD_DIM, HEAD_DIM), F32)],
        compiler_params=_params(1),
        name="deltanet_scan",
    )(egl, u, w, qg, kd, intra, z, dn_norm.reshape(1, HEAD_DIM).astype(F32))


def _proj_norm_kernel(p_ref, w_ref, g_ref, o_ref, wcast_ref):
    @pl.when(pl.program_id(0) == 0)
    def _():
        wcast_ref[...] = w_ref[...].astype(BF16)

    y = jnp.dot(p_ref[...].astype(BF16), wcast_ref[...], preferred_element_type=F32)
    ms = jnp.mean(y * y, axis=-1, keepdims=True)
    o_ref[...] = (y * lax.rsqrt(ms + EPS) * g_ref[...]).astype(o_ref.dtype)


def proj_norm(p, w, g, out_dtype, *, tm=512):
    m, kdim = p.shape
    n = w.shape[1]
    return pl.pallas_call(
        _proj_norm_kernel,
        grid=(m // tm,),
        in_specs=[pl.BlockSpec((tm, kdim), lambda i: (i, 0)),
                  pl.BlockSpec((kdim, n), lambda i: (0, 0)),
                  pl.BlockSpec((1, n), lambda i: (0, 0))],
        out_specs=pl.BlockSpec((tm, n), lambda i: (i, 0)),
        out_shape=jax.ShapeDtypeStruct((m, n), out_dtype),
        scratch_shapes=[pltpu.VMEM((kdim, n), BF16)],
        compiler_params=_params(1),
        name="proj_norm",
    )(p, w, g.reshape(1, n).astype(F32))


def _layer(x, p, w_in, conv_w, a_log, dt_bias, dn_norm, w_attn_up, w_dn_up, w_out, w_mlp_up,
           w_mlp_down, w_ple_gate, w_ple_proj, norm_mix, norm_mlp, norm_ple, ple_post_norm):
    s_len, d_model = x.shape
    attn_w = ATTN_HEADS * HEAD_DIM
    dn_w = DN_HEADS * HEAD_DIM
    n_groups = len(ATTN_DILATIONS)
    c_qkv_b = 3 * n_groups * attn_w
    c_z = c_qkv_b + 3 * dn_w
    c_ba = c_z + dn_w
    c_gate = c_ba + 2 * DN_HEADS
    mm = matmul_ws

    w_in_t = w_in.T
    h = rmsnorm(x, norm_mix, BF16)
    max_dil = max(ATTN_DILATIONS)
    qkv_g0 = mm(h, w_in_t, b_is_nk=True, n=3 * attn_w, tn=attn_w, out_dtype=BF16,
                panel_col=lambda j: j * (n_groups * attn_w), name="proj_attn_g0")
    qkv_g12 = mm(h, w_in_t, b_is_nk=True, n=6 * attn_w, tn=attn_w, out_dtype=BF16,
                 epilogue="plain_by_residue", residues=max_dil,
                 panel_col=lambda j: (j // 2) * (n_groups * attn_w) + (1 + j % 2) * attn_w,
                 name="proj_attn_g12")
    qk_b = mm(h, w_in_t, b_is_nk=True, col0=c_qkv_b, n=2 * dn_w, tn=1024, out_dtype=BF16,
              epilogue="conv_silu_l2norm", extra=((conv_w, 0),), name="proj_dn_qk")
    v_b = mm(h, w_in_t, b_is_nk=True, col0=c_qkv_b + 2 * dn_w, n=dn_w, tn=1024, out_dtype=BF16,
             epilogue="conv_silu", extra=((conv_w, 2 * dn_w),), name="proj_dn_v")
    z_b = mm(h, w_in_t, b_is_nk=True, col0=c_z, n=dn_w, tn=1024, out_dtype=BF16, name="proj_z")
    ba = mm(h, w_in_t, b_is_nk=True, col0=c_ba, n=LANES, tn=LANES, out_dtype=F32, name="proj_ba")
    gates = mm(h, w_in_t, b_is_nk=True, col0=c_gate, n=2 * d_model, tn=1024, out_dtype=BF16,
               epilogue="sigmoid", name="proj_gates")

    g0 = qkv_g0.reshape(1, 1, s_len, 3 * attn_w)
    o0, lse0 = banded_attention(g0, g0, g0, col_blocks=(0, 1, 2))
    per = s_len // max_dil
    g1 = qkv_g12.reshape(4, 4, per, 6 * attn_w)
    o1, lse1 = banded_attention(g1, g1, g1, col_blocks=(0, 2, 4))
    g2 = qkv_g12.reshape(1, max_dil, per, 6 * attn_w)
    o2, lse2 = banded_attention(g2, g2, g2, col_blocks=(1, 3, 5))

    def natural(t):
        return t.transpose(2, 0, 1, 3).reshape(s_len, t.shape[-1])

    o_a = attn_combine([o0.reshape(s_len, attn_w), natural(o1), natural(o2)],
                       [lse0.reshape(s_len, HEAD_DIM), natural(lse1), natural(lse2)])

    o_b = gated_deltanet2(qk_b, v_b, ba, a_log, dt_bias, z_b, dn_norm)

    t_a = mm(o_a, w_attn_up, tn=1024, out_dtype=BF16, epilogue="mul", extra=((gates, 0),),
             name="attn_up")
    merged = mm(o_b, w_dn_up, tn=1024, out_dtype=BF16, epilogue="mul_add",
                extra=((gates, d_model), (t_a, 0)), name="dn_up")
    x = mm(merged, w_out, tm=512, tn=1024, out_dtype=F32, epilogue="residual", extra=((x, 0),),
           name="out_proj")

    h = rmsnorm(x, norm_mlp, BF16)
    u = mm(h, w_mlp_up, tn=1024, out_dtype=BF16, epilogue="relu2", name="mlp_up")
    x = mm(u, w_mlp_down, out_dtype=F32, epilogue="residual", extra=((x, 0),), tk=4096,
           name="mlp_down")

    h = rmsnorm(x, norm_ple, BF16)
    pp = proj_norm(p, w_ple_proj, ple_post_norm, BF16)
    x = mm(h, w_ple_gate, tm=512, tn=1024, out_dtype=F32, epilogue="ple",
           extra=((x, 0), (pp, 0)), name="ple_gate")
    return x


def kernel(x, p, w_in, conv_w, dn_a_log, dn_dt_bias, dn_norm, w_attn_up, w_dn_up, w_out,
           w_mlp_up, w_mlp_down, w_ple_gate, w_ple_proj, norm_mix, norm_mlp, norm_ple,
           ple_post_norm, final_norm):
    b, s_len, d_model = x.shape
    assert b == 1
    depth = w_in.shape[0]
    xs = x.reshape(s_len, d_model)
    for i in range(depth):
        xs = _layer(xs, p[i, 0], w_in[i], conv_w[i], dn_a_log[i], dn_dt_bias[i], dn_norm[i],
                    w_attn_up[i], w_dn_up[i], w_out[i], w_mlp_up[i], w_mlp_down[i],
                    w_ple_gate[i], w_ple_proj[i], norm_mix[i], norm_mlp[i], norm_ple[i],
                    ple_post_norm[i])
    out = rmsnorm(xs, final_norm, x.dtype)
    return out.reshape(b, s_len, d_model)
```

```python
import functools
import math

import jax
import jax.numpy as jnp
from jax import lax
from jax.experimental import pallas as pl
from jax.experimental.pallas import tpu as pltpu

F32 = jnp.float32
BF16 = jnp.bfloat16

EPS = 1e-6
LANES = 128
SUBLANES = 8
MXU_WIDTH = 256
ROW_SUB = 256
HEAD_DIM = 128
ATTN_SPAN = 128
ATTN_DILATIONS = (1, 4, 16)
ATTN_HEADS = 8
DN_HEADS = 16
DN_CHUNK = 64
CONV_K = 4

VMEM_LIMIT_BYTES = 56 * 1024 * 1024


def _params(n_axes):
    return pltpu.CompilerParams(
        dimension_semantics=("arbitrary",) * n_axes,
        vmem_limit_bytes=VMEM_LIMIT_BYTES,
    )


def _rmsnorm_kernel(x_ref, g_ref, o_ref):
    x = x_ref[...].astype(F32)
    ms = jnp.mean(x * x, axis=-1, keepdims=True)
    o_ref[...] = (x * lax.rsqrt(ms + EPS) * g_ref[...]).astype(o_ref.dtype)


def rmsnorm(x, g, out_dtype, *, tm=512):
    m, d = x.shape
    return pl.pallas_call(
        _rmsnorm_kernel,
        grid=(m // tm,),
        in_specs=[pl.BlockSpec((tm, d), lambda i: (i, 0)),
                  pl.BlockSpec((1, d), lambda i: (0, 0))],
        out_specs=pl.BlockSpec((tm, d), lambda i: (i, 0)),
        out_shape=jax.ShapeDtypeStruct((m, d), out_dtype),
        compiler_params=_params(1),
        name="rmsnorm",
    )(x, g.reshape(1, d).astype(F32))


def _sigmoid(x):
    return 1.0 / (1.0 + jnp.exp(-x))


def _mm_kernel(*refs, nk, tm, epilogue, n_extra, b_is_nk):
    a_ref, bchunk_ref = refs[0], refs[1]
    extra = refs[2:2 + n_extra]
    o_ref = refs[2 + n_extra]
    w_even, w_odd = refs[3 + n_extra], refs[4 + n_extra]
    acc_ref = refs[5 + n_extra]
    t = pl.program_id(0)
    i = pl.program_id(1)
    chunk_rows = bchunk_ref.shape[0]

    def stage(w_ref):
        rows = pl.ds(pl.multiple_of(i * chunk_rows, chunk_rows), chunk_rows)
        w_ref[rows, :] = bchunk_ref[...].astype(BF16)

    def compute(w_ref):
        _mm_compute(a_ref, w_ref, extra, o_ref, acc_ref, t - 1, i, nk=nk, tm=tm,
                    epilogue=epilogue, b_is_nk=b_is_nk)

    @pl.when(t == 0)
    def _():
        stage(w_even)

    @pl.when(jnp.logical_and(t > 0, t % 2 == 1))
    def _():
        stage(w_odd)
        compute(w_even)

    @pl.when(jnp.logical_and(t > 0, t % 2 == 0))
    def _():
        stage(w_even)
        compute(w_odd)


def _mm_compute(a_ref, w_ref, extra, o_ref, acc_ref, tile, i, *, nk, tm, epilogue, b_is_nk):
    k = tile % nk
    tn = o_ref.shape[-1]
    strip = min(MXU_WIDTH, tn)
    conv = epilogue.startswith("conv_silu")

    def product(row_slice, cols):
        if b_is_nk:
            return lax.dot_general(a_ref[row_slice, :], w_ref[cols, :], (((1,), (1,)), ((), ())),
                                   preferred_element_type=F32)
        return jnp.dot(a_ref[row_slice, :], w_ref[:, cols], preferred_element_type=F32)

    if not conv and epilogue != "plain_by_residue":
        row_sub = min(ROW_SUB, tm)
        for rs in range(tm // row_sub):
            rr = slice(rs * row_sub, (rs + 1) * row_sub)
            for c in range(tn // strip):
                cols = slice(c * strip, (c + 1) * strip)
                acc = product(rr, cols)
                if nk > 1:
                    arows = pl.ds(pl.multiple_of(i * tm + rs * row_sub, row_sub), row_sub)
                    acc = acc + jnp.where(k == 0, 0.0, acc_ref[arows, cols])
                    acc_ref[arows, cols] = acc
                if epilogue == "plain":
                    val = acc
                elif epilogue == "relu2":
                    r = jnp.maximum(acc, 0.0)
                    val = r * r
                elif epilogue == "sigmoid":
                    val = _sigmoid(acc)
                elif epilogue == "residual":
                    val = extra[0][rr, cols].astype(F32) + acc
                elif epilogue == "mul":
                    val = extra[0][rr, cols].astype(F32) * acc
                elif epilogue == "mul_add":
                    val = extra[1][rr, cols].astype(F32) + extra[0][rr, cols].astype(F32) * acc
                elif epilogue == "ple":
                    val = (extra[0][rr, cols].astype(F32)
                           + _sigmoid(acc) * extra[1][rr, cols].astype(F32))
                else:
                    raise ValueError(epilogue)
                o_ref[rr, cols] = val.astype(o_ref.dtype)
        return

    assert nk == 1
    all_rows = slice(0, tm)
    if epilogue == "plain_by_residue":
        for c in range(tn // strip):
            acc = product(all_rows, slice(c * strip, (c + 1) * strip))
            for piece in range(strip // LANES):
                acc_ref[c * (strip // LANES) + piece] = acc[:, piece * LANES:(piece + 1) * LANES]
        n_res = o_ref.shape[0]
        for piece in range(tn // LANES):
            for r in range(n_res):
                o_ref[r, :, piece * LANES:(piece + 1) * LANES] = acc_ref.at[piece][
                    pl.ds(r, tm // n_res, stride=n_res), :].astype(o_ref.dtype)
        return

    halo = SUBLANES

    @pl.when(i == 0)
    def _():
        acc_ref[0:halo, :] = jnp.zeros((halo, tn), F32)

    for c in range(tn // strip):
        cols = slice(c * strip, (c + 1) * strip)
        acc_ref[halo:halo + tm, cols] = product(all_rows, cols)
    for c in range(tn // strip):
        cols = slice(c * strip, (c + 1) * strip)
        acc = acc_ref[halo:halo + tm, cols]
        cw = extra[0][:, cols]
        base = halo - (CONV_K - 1)
        y = acc * cw[CONV_K - 1:CONV_K]
        for tap in range(CONV_K - 1):
            y = y + acc_ref[base + tap:base + tap + tm, cols] * cw[tap:tap + 1]
        acc_ref[0:halo, cols] = acc[tm - halo:, :]
        y = y * _sigmoid(y)
        if epilogue == "conv_silu_l2norm":
            segs = []
            for hh in range(strip // HEAD_DIM):
                seg = y[:, hh * HEAD_DIM:(hh + 1) * HEAD_DIM]
                segs.append(seg * lax.rsqrt(jnp.sum(seg * seg, axis=-1, keepdims=True) + EPS))
            y = jnp.concatenate(segs, axis=1)
        o_ref[:, cols] = y.astype(o_ref.dtype)


def matmul_ws(a, b, *, col0=0, n=None, out_dtype=BF16, epilogue="plain", extra=(),
              tm=1024, tn=512, tk=None, b_is_nk=False, panel_col=None, residues=1,
              name="matmul_ws"):
    m, kdim = a.shape
    n_total = b.shape[0] if b_is_nk else b.shape[1]
    if n is None:
        n = n_total - col0
    if tk is None:
        tk = kdim
    assert m % tm == 0 and n % tn == 0 and kdim % tk == 0 and col0 + n <= n_total
    assert panel_col is None or b_is_nk
    nj, nk, ni = n // tn, kdim // tk, m // tm
    n_tiles = nj * nk

    def tile_jk(t):
        tile = jnp.maximum(t - 1, 0)
        return tile // nk, tile % nk

    def row_block(t, i):
        _, k = tile_jk(t)
        return jnp.where(jnp.logical_and(t > 0, k == nk - 1), i, 0)

    def a_map(t, i):
        return (jnp.where(t > 0, i, 0), tile_jk(t)[1])

    def out_map(t, i):
        return (row_block(t, i), tile_jk(t)[0])

    def staged(t, i):
        tile = jnp.minimum(t, n_tiles - 1)
        return tile // nk, tile % nk, jnp.where(t == n_tiles, ni - 1, i)

    if b_is_nk:
        assert col0 % SUBLANES == 0 and tn % (ni * SUBLANES) == 0
        cn = tn // ni

        def b_map(t, i):
            j, k, c = staged(t, i)
            first = col0 + j * tn if panel_col is None else panel_col(j)
            return (pl.multiple_of(first + c * cn, SUBLANES), pl.multiple_of(k * tk, LANES))

        b_spec = pl.BlockSpec((pl.Element(cn), pl.Element(tk)), b_map)
        bcast_shape = (tn, tk)
    else:
        assert col0 % tn == 0 and tk % (ni * SUBLANES) == 0

        def b_map(t, i):
            j, k, c = staged(t, i)
            return (k * ni + c, j + col0 // tn)

        b_spec = pl.BlockSpec((tk // ni, tn), b_map)
        bcast_shape = (tk, tn)
    in_specs = [pl.BlockSpec((tm, tk), a_map), b_spec]
    conv = epilogue.startswith("conv_silu")
    for idx, (_, ecol0) in enumerate(extra):
        assert ecol0 % tn == 0
        if conv and idx == 0:
            in_specs.append(pl.BlockSpec(
                (CONV_K, tn),
                functools.partial(lambda t, i, e0: (0, tile_jk(t)[0] + e0), e0=ecol0 // tn)))
        else:
            in_specs.append(pl.BlockSpec(
                (tm, tn),
                functools.partial(lambda t, i, e0: (row_block(t, i), tile_jk(t)[0] + e0),
                                  e0=ecol0 // tn)))
    if epilogue == "plain_by_residue":
        assert nk == 1 and tm % (residues * SUBLANES) == 0
        out_spec = pl.BlockSpec((residues, tm // residues, tn),
                                lambda t, i: (0, row_block(t, i), tile_jk(t)[0]))
        out_shape = jax.ShapeDtypeStruct((residues, m // residues, n), out_dtype)
    else:
        out_spec = pl.BlockSpec((tm, tn), out_map)
        out_shape = jax.ShapeDtypeStruct((m, n), out_dtype)
    scratch = [pltpu.VMEM(bcast_shape, BF16), pltpu.VMEM(bcast_shape, BF16)]
    if conv:
        assert nk == 1
        scratch.append(pltpu.VMEM((SUBLANES + tm, tn), F32))
    elif epilogue == "plain_by_residue":
        scratch.append(pltpu.VMEM((tn // LANES, tm, LANES), F32))
    else:
        scratch.append(pltpu.VMEM((m, tn) if nk > 1 else (SUBLANES, LANES), F32))
    return pl.pallas_call(
        functools.partial(_mm_kernel, nk=nk, tm=tm, epilogue=epilogue, n_extra=len(extra),
                          b_is_nk=b_is_nk),
        grid=(n_tiles + 1, ni),
        in_specs=in_specs,
        out_specs=out_spec,
        out_shape=out_shape,
        scratch_shapes=scratch,
        compiler_params=_params(2),
        name=name,
    )(a, b, *[arr for arr, _ in extra])


def _attn_kernel(q_ref, kp_ref, kc_ref, vp_ref, vc_ref, o_ref, lse_ref, *, heads, scale):
    n = pl.program_id(1)
    planes, sub, _ = q_ref.shape
    span = planes * sub
    rho = lax.broadcasted_iota(jnp.int32, (span, 2 * span), 0)
    kap = lax.broadcasted_iota(jnp.int32, (span, 2 * span), 1)
    in_cur = kap // span
    dist = (planes * (sub * (1 - in_cur) + rho % sub - kap % sub)
            + rho // sub - (kap % span) // sub)
    valid = (dist >= 0) & (dist <= span) & ((in_cur == 1) | (n > 0))
    lane = lax.broadcasted_iota(jnp.int32, (span, HEAD_DIM), 1)
    lse_tile = jnp.zeros((span, HEAD_DIM), F32)

    def tile(ref, cols):
        return jnp.concatenate([ref[a, :, cols] for a in range(planes)], axis=0)

    for h in range(heads):
        cols = slice(h * HEAD_DIM, (h + 1) * HEAD_DIM)
        q = tile(q_ref, cols)
        k = jnp.concatenate([tile(kp_ref, cols), tile(kc_ref, cols)], axis=0)
        v = jnp.concatenate([tile(vp_ref, cols), tile(vc_ref, cols)], axis=0)
        s = lax.dot_general(q, k, (((1,), (1,)), ((), ())), preferred_element_type=F32) * scale
        s = jnp.where(valid, s, -jnp.inf)
        m = jnp.max(s, axis=-1, keepdims=True)
        e = jnp.exp(s - m)
        l = jnp.sum(e, axis=-1, keepdims=True)
        o = (jnp.dot(e.astype(BF16), v, preferred_element_type=F32) / l).astype(o_ref.dtype)
        for a in range(planes):
            o_ref[a, :, cols] = o[a * sub:(a + 1) * sub]
        lse_tile = jnp.where(lane == h, m + jnp.log(l), lse_tile)
    for a in range(planes):
        lse_ref[a] = lse_tile[a * sub:(a + 1) * sub]


def banded_attention(q, k, v, *, heads=ATTN_HEADS, col_blocks=(0, 0, 0)):
    planes, n_sub, rows, _ = q.shape
    width = heads * HEAD_DIM
    sub = ATTN_SPAN // planes
    nb = rows // sub

    def cur_spec(cb):
        return pl.BlockSpec((planes, None, sub, width), lambda r, n: (0, r, n, cb))

    def prev_spec(cb):
        return pl.BlockSpec((planes, None, sub, width),
                            lambda r, n: (0, r, jnp.maximum(n - 1, 0), cb))

    qc, kc, vc = col_blocks
    return pl.pallas_call(
        functools.partial(_attn_kernel, heads=heads, scale=HEAD_DIM ** -0.5),
        grid=(n_sub, nb),
        in_specs=[cur_spec(qc), prev_spec(kc), cur_spec(kc), prev_spec(vc), cur_spec(vc)],
        out_specs=[pl.BlockSpec((planes, None, sub, width), lambda r, n: (0, r, n, 0)),
                   pl.BlockSpec((planes, None, sub, HEAD_DIM), lambda r, n: (0, r, n, 0))],
        out_shape=[jax.ShapeDtypeStruct((planes, n_sub, rows, width), BF16),
                   jax.ShapeDtypeStruct((planes, n_sub, rows, HEAD_DIM), F32)],
        compiler_params=_params(2),
        name="banded_attention",
    )(q, k, k, v, v)


def _attn_combine_kernel(o0_ref, o1_ref, o2_ref, l0_ref, l1_ref, l2_ref, out_ref, *, heads):
    l0, l1, l2 = l0_ref[...], l1_ref[...], l2_ref[...]
    m = jnp.maximum(jnp.maximum(l0, l1), l2)
    e0, e1, e2 = jnp.exp(l0 - m), jnp.exp(l1 - m), jnp.exp(l2 - m)
    den = e0 + e1 + e2
    w0, w1, w2 = e0 / den, e1 / den, e2 / den
    for h in range(heads):
        cols = slice(h * HEAD_DIM, (h + 1) * HEAD_DIM)
        acc = (w0[:, h:h + 1] * o0_ref[:, cols] + w1[:, h:h + 1] * o1_ref[:, cols]
               + w2[:, h:h + 1] * o2_ref[:, cols])
        out_ref[:, cols] = acc.astype(out_ref.dtype)


def attn_combine(outs, lses, *, tm=256):
    m, width = outs[0].shape
    heads = width // HEAD_DIM
    ospec = pl.BlockSpec((tm, width), lambda i: (i, 0))
    lspec = pl.BlockSpec((tm, HEAD_DIM), lambda i: (i, 0))
    return pl.pallas_call(
        functools.partial(_attn_combine_kernel, heads=heads),
        grid=(m // tm,),
        in_specs=[ospec] * 3 + [lspec] * 3,
        out_specs=ospec,
        out_shape=jax.ShapeDtypeStruct((m, width), BF16),
        compiler_params=_params(1),
        name="attn_combine",
    )(*outs, *lses)


def _softplus(x):
    return jnp.maximum(x, 0.0) + jnp.log(1.0 + jnp.exp(-jnp.abs(x)))


def _bdot(a, b, dims):
    return lax.dot_general(a, b, (dims, ((0,), (0,))), preferred_element_type=F32)


def _bmm(a, b):
    return _bdot(a, b, ((2,), (1,)))


def _split_bf16(x):
    hi = x.astype(BF16)
    lo = (x - hi.astype(F32)).astype(BF16)
    return hi, lo


def _bmm_split(a, b):
    a_hi, a_lo = _split_bf16(a)
    b_hi, b_lo = _split_bf16(b)
    return _bmm(a_hi, b_hi) + _bmm(a_hi, b_lo) + _bmm(a_lo, b_hi)


def _unit_lower_inverse_batched(a):
    c = a.shape[-1]
    b0 = SUBLANES
    ri = lax.broadcasted_iota(jnp.int32, (c, c), 0)
    ci = lax.broadcasted_iota(jnp.int32, (c, c), 1)
    below = ri > ci
    eye = (ri == ci).astype(F32)

    def same_block(b):
        return (ri // b) == (ci // b)

    def mm(x, y):
        return _bmm(x.astype(BF16), y.astype(BF16))

    d = jnp.where(same_block(b0) & below, a, 0.0)
    p = eye - d
    x = d
    for _ in range(int(math.log2(b0)) - 1):
        x = mm(x, x)
        p = p + mm(p, x)
    b = b0
    while b < c:
        off_diag = jnp.where(same_block(2 * b) & jnp.logical_not(same_block(b)) & below, a, 0.0)
        p = p - mm(mm(p, off_diag), p)
        b *= 2
    resid = eye - p - _bmm_split(a, p)
    return p + mm(p, resid)


def _dn_prep_kernel(q_ref, k_ref, v_ref, ba_ref, bat_ref, prow_ref, pcol_ref,
                    u_ref, w_ref, qg_ref, kd_ref, intra_ref, eg_ref, *, heads, chunks):
    c_len = DN_CHUNK

    ba = ba_ref[...]
    beta_col = _sigmoid(ba)
    g_col = -jnp.exp(prow_ref[0:1, :]) * _softplus(ba + prow_ref[1:2, :])
    row_in_chunk = lax.broadcasted_iota(jnp.int32, g_col.shape, 0) % c_len
    shift = 1
    while shift < c_len:
        g_col = g_col + jnp.where(row_in_chunk >= shift, pltpu.roll(g_col, shift, 0), 0.0)
        shift *= 2
    bat = bat_ref[...]
    beta_row = _sigmoid(bat)
    g_row = -jnp.exp(pcol_ref[:, 0:1])[None] * _softplus(bat + pcol_ref[:, 1:2][None])
    lane_in_chunk = lax.broadcasted_iota(jnp.int32, g_row.shape, 2)
    shift = 1
    while shift < c_len:
        g_row = g_row + jnp.where(lane_in_chunk >= shift, pltpu.roll(g_row, shift, 2), 0.0)
        shift *= 2
    eg_ref[...] = jnp.exp(g_row)

    ri = lax.broadcasted_iota(jnp.int32, (c_len, c_len), 0)
    ci = lax.broadcasted_iota(jnp.int32, (c_len, c_len), 1)
    lower = ri >= ci
    strict = ri > ci

    units = [(c, h) for c in range(chunks) for h in range(heads)]

    def rows_of(c):
        return slice(c * c_len, (c + 1) * c_len)

    def per_unit(x_ref):
        return jnp.stack([x_ref[rows_of(c), h * HEAD_DIM:(h + 1) * HEAD_DIM] for c, h in units])

    q = per_unit(q_ref).astype(F32) * HEAD_DIM ** -0.5
    k16 = per_unit(k_ref)
    k = k16.astype(F32)
    v16 = per_unit(v_ref)
    gcol = jnp.stack([g_col[rows_of(c), heads + h:heads + h + 1] for c, h in units])
    bcol = jnp.stack([beta_col[rows_of(c), h:h + 1] for c, h in units])
    grow = jnp.stack([g_row[c, heads + h:heads + h + 1, :] for c, h in units])
    brow = jnp.stack([beta_row[c, h:h + 1, :] for c, h in units])
    decay = jnp.exp(jnp.where(lower, gcol - grow, -jnp.inf))
    q16 = q.astype(BF16)
    qk = _bdot(jnp.concatenate([q16, k16], axis=1), k16, ((2,), (2,)))
    intra = qk[:, :c_len] * decay
    a = jnp.where(strict, qk[:, c_len:] * decay * bcol, 0.0)
    t_inv = _unit_lower_inverse_batched(a)
    u = _bmm((t_inv * brow).astype(BF16), v16)
    w = _bmm((t_inv * (brow * jnp.exp(grow))).astype(BF16), k16)
    g_last = grow[:, :, c_len - 1:c_len]
    qg = q * jnp.exp(gcol)
    kd = k * jnp.exp(g_last - gcol)
    for c in range(chunks):
        rs, sl = rows_of(c), slice(c * heads, (c + 1) * heads)
        u_ref[:, rs, :] = u[sl]
        w_ref[:, rs, :] = w[sl].astype(w_ref.dtype)
        qg_ref[:, rs, :] = qg[sl].astype(qg_ref.dtype)
        kd_ref[:, rs, :] = kd[sl].astype(kd_ref.dtype)
        intra_ref[:, rs, :] = intra[sl].astype(intra_ref.dtype)


def _dn_scan_kernel(egl_ref, u_ref, w_ref, qg_ref, kd_ref, intra_ref, z_ref, gn_ref, o_ref,
                    state_ref, *, heads, chunks):
    tb = pl.program_id(0)
    c_len = DN_CHUNK

    @pl.when(tb == 0)
    def _():
        state_ref[...] = jnp.zeros_like(state_ref)

    gn = gn_ref[...]
    for c in range(chunks):
        rs = slice(c * c_len, (c + 1) * c_len)
        s = state_ref[...]
        lhs = jnp.concatenate([w_ref[:, rs, :], qg_ref[:, rs, :]], axis=1)
        r = _bmm(lhs, s.astype(BF16))
        v_new = u_ref[:, rs, :] - r[:, :c_len]
        vn16 = v_new.astype(BF16)
        o = r[:, c_len:] + _bmm(intra_ref[:, rs, :], vn16)
        upd = _bdot(kd_ref[:, rs, :], vn16, ((1,), (1,)))
        base = (tb * chunks + c) * heads
        for h in range(heads):
            state_ref[h] = s[h] * egl_ref[base + h] + upd[h]
        o = o * lax.rsqrt(jnp.mean(o * o, axis=-1, keepdims=True) + EPS) * gn
        for h in range(heads):
            cs = slice(h * HEAD_DIM, (h + 1) * HEAD_DIM)
            zz = z_ref[rs, cs].astype(F32)
            o_ref[rs, cs] = (o[h] * (zz * _sigmoid(zz))).astype(o_ref.dtype)


def gated_deltanet2(qk, v, ba, a_log, dt_bias, z, dn_norm, *, prep_chunks=4, scan_chunks=4):
    s_len, width = v.shape
    heads = width // HEAD_DIM
    c_len = DN_CHUNK
    n_chunks = s_len // c_len
    bat = ba[:, :2 * heads].reshape(n_chunks, c_len, 2 * heads).transpose(0, 2, 1)
    prow = jnp.zeros((2, HEAD_DIM), F32)
    prow = prow.at[0, heads:2 * heads].set(a_log.astype(F32))
    prow = prow.at[1, heads:2 * heads].set(dt_bias.astype(F32))
    pcol = jnp.zeros((2 * heads, 2), F32)
    pcol = pcol.at[heads:, 0].set(a_log.astype(F32)).at[heads:, 1].set(dt_bias.astype(F32))

    rows = prep_chunks * c_len

    def main_spec(i):
        return pl.BlockSpec((rows, width), lambda t: (t, i))

    def prep_out(last):
        return pl.BlockSpec((heads, rows, last), lambda t: (0, t, 0))

    u, w, qg, kd, intra, eg = pl.pallas_call(
        functools.partial(_dn_prep_kernel, heads=heads, chunks=prep_chunks),
        grid=(s_len // rows,),
        in_specs=[main_spec(0), main_spec(1), main_spec(0),
                  pl.BlockSpec((rows, HEAD_DIM), lambda t: (t, 0)),
                  pl.BlockSpec((prep_chunks, 2 * heads, c_len), lambda t: (t, 0, 0)),
                  pl.BlockSpec((2, HEAD_DIM), lambda t: (0, 0)),
                  pl.BlockSpec((2 * heads, 2), lambda t: (0, 0))],
        out_specs=[prep_out(HEAD_DIM), prep_out(HEAD_DIM), prep_out(HEAD_DIM),
                   prep_out(HEAD_DIM), prep_out(c_len),
                   pl.BlockSpec((prep_chunks, 2 * heads, c_len), lambda t: (t, 0, 0))],
        out_shape=[jax.ShapeDtypeStruct((heads, s_len, HEAD_DIM), F32),
                   jax.ShapeDtypeStruct((heads, s_len, HEAD_DIM), BF16),
                   jax.ShapeDtypeStruct((heads, s_len, HEAD_DIM), BF16),
                   jax.ShapeDtypeStruct((heads, s_len, HEAD_DIM), BF16),
                   jax.ShapeDtypeStruct((heads, s_len, c_len), BF16),
                   jax.ShapeDtypeStruct((n_chunks, 2 * heads, c_len), F32)],
        compiler_params=_params(1),
        name="deltanet_prep",
    )(qk, qk, v, ba, bat, prow, pcol)

    egl = eg[:, heads:, c_len - 1].reshape(n_chunks * heads)

    scan_rows = scan_chunks * c_len

    def head_major(last):
        return pl.BlockSpec((heads, scan_rows, last), lambda t: (0, t, 0))

    nat = pl.BlockSpec((scan_rows, width), lambda t: (t, 0))
    return pl.pallas_call(
        functools.partial(_dn_scan_kernel, heads=heads, chunks=scan_chunks),
        grid=(s_len // scan_rows,),
        in_specs=[pl.BlockSpec(memory_space=pltpu.SMEM),
                  head_major(HEAD_DIM), head_major(HEAD_DIM), head_major(HEAD_DIM),
                  head_major(HEAD_DIM), head_major(c_len), nat,
                  pl.BlockSpec((1, HEAD_DIM), lambda t: (0, 0))],
        out_specs=nat,
        out_shape=jax.ShapeDtypeStruct((s_len, width), BF16),
        scratch_shapes=[pltpu.VMEM((heads, HEAD_DIM, HEAD_DIM), F32)],
        compiler_params=_params(1),
        name="deltanet_scan",
    )(egl, u, w, qg, kd, intra, z, dn_norm.reshape(1, HEAD_DIM).astype(F32))


def _proj_norm_kernel(p_ref, w_ref, g_ref, o_ref, wcast_ref):
    @pl.when(pl.program_id(0) == 0)
    def _():
        wcast_ref[...] = w_ref[...].astype(BF16)

    y = jnp.dot(p_ref[...].astype(BF16), wcast_ref[...], preferred_element_type=F32)
    ms = jnp.mean(y * y, axis=-1, keepdims=True)
    o_ref[...] = (y * lax.rsqrt(ms + EPS) * g_ref[...]).astype(o_ref.dtype)


def proj_norm(p, w, g, out_dtype, *, tm=512):
    m, kdim = p.shape
    n = w.shape[1]
    return pl.pallas_call(
        _proj_norm_kernel,
        grid=(m // tm,),
        in_specs=[pl.BlockSpec((tm, kdim), lambda i: (i, 0)),
                  pl.BlockSpec((kdim, n), lambda i: (0, 0)),
                  pl.BlockSpec((1, n), lambda i: (0, 0))],
        out_specs=pl.BlockSpec((tm, n), lambda i: (i, 0)),
        out_shape=jax.ShapeDtypeStruct((m, n), out_dtype),
        scratch_shapes=[pltpu.VMEM((kdim, n), BF16)],
        compiler_params=_params(1),
        name="proj_norm",
    )(p, w, g.reshape(1, n).astype(F32))


def _layer(x, p, w_in, conv_w, a_log, dt_bias, dn_norm, w_attn_up, w_dn_up, w_out, w_mlp_up,
           w_mlp_down, w_ple_gate, w_ple_proj, norm_mix, norm_mlp, norm_ple, ple_post_norm):
    s_len, d_model = x.shape
    attn_w = ATTN_HEADS * HEAD_DIM
    dn_w = DN_HEADS * HEAD_DIM
    n_groups = len(ATTN_DILATIONS)
    c_qkv_b = 3 * n_groups * attn_w
    c_z = c_qkv_b + 3 * dn_w
    c_ba = c_z + dn_w
    c_gate = c_ba + 2 * DN_HEADS
    mm = matmul_ws

    w_in_t = w_in.T
    h = rmsnorm(x, norm_mix, BF16)
    max_dil = max(ATTN_DILATIONS)
    z_panels = dn_w // attn_w
    z_qkv_g0 = mm(h, w_in_t, b_is_nk=True, n=dn_w + 3 * attn_w, tn=attn_w, out_dtype=BF16,
                  panel_col=lambda j: jnp.where(j < z_panels, c_z + j * attn_w,
                                                (j - z_panels) * (n_groups * attn_w)),
                  name="proj_z_attn_g0")
    qkv_g12 = mm(h, w_in_t, b_is_nk=True, n=6 * attn_w, tn=attn_w, out_dtype=BF16,
                 epilogue="plain_by_residue", residues=max_dil,
                 panel_col=lambda j: (j // 2) * (n_groups * attn_w) + (1 + j % 2) * attn_w,
                 name="proj_attn_g12")
    qk_b = mm(h, w_in_t, b_is_nk=True, col0=c_qkv_b, n=2 * dn_w, tn=1024, out_dtype=BF16,
              epilogue="conv_silu_l2norm", extra=((conv_w, 0),), name="proj_dn_qk")
    v_b = mm(h, w_in_t, b_is_nk=True, col0=c_qkv_b + 2 * dn_w, n=dn_w, tn=1024, out_dtype=BF16,
             epilogue="conv_silu", extra=((conv_w, 2 * dn_w),), name="proj_dn_v")
    ba = mm(h, w_in_t, b_is_nk=True, col0=c_ba, n=LANES, tm=2048, tn=LANES, out_dtype=F32,
            name="proj_ba")
    gates = mm(h, w_in_t, b_is_nk=True, col0=c_gate, n=2 * d_model, tn=1024, out_dtype=BF16,
               epilogue="sigmoid", name="proj_gates")

    g0 = z_qkv_g0.reshape(1, 1, s_len, dn_w + 3 * attn_w)
    o0, lse0 = banded_attention(g0, g0, g0,
                                col_blocks=(z_panels, z_panels + 1, z_panels + 2))
    per = s_len // max_dil
    g1 = qkv_g12.reshape(4, 4, per, 6 * attn_w)
    o1, lse1 = banded_attention(g1, g1, g1, col_blocks=(0, 2, 4))
    g2 = qkv_g12.reshape(1, max_dil, per, 6 * attn_w)
    o2, lse2 = banded_attention(g2, g2, g2, col_blocks=(1, 3, 5))

    def natural(t):
        return t.transpose(2, 0, 1, 3).reshape(s_len, t.shape[-1])

    o_a = attn_combine([o0.reshape(s_len, attn_w), natural(o1), natural(o2)],
                       [lse0.reshape(s_len, HEAD_DIM), natural(lse1), natural(lse2)])

    o_b = gated_deltanet2(qk_b, v_b, ba, a_log, dt_bias, z_qkv_g0, dn_norm)

    t_a = mm(o_a, w_attn_up, tm=2048, tn=1024, out_dtype=BF16, epilogue="mul",
             extra=((gates, 0),), name="attn_up")
    merged = mm(o_b, w_dn_up, tm=2048, tn=1024, out_dtype=BF16, epilogue="mul_add",
                extra=((gates, d_model), (t_a, 0)), name="dn_up")
    x = mm(merged, w_out, tm=512, tn=1024, out_dtype=F32, epilogue="residual", extra=((x, 0),),
           name="out_proj")

    h = rmsnorm(x, norm_mlp, BF16)
    u = mm(h, w_mlp_up, tn=1024, out_dtype=BF16, epilogue="relu2", name="mlp_up")
    x = mm(u, w_mlp_down, out_dtype=F32, epilogue="residual", extra=((x, 0),), tk=4096,
           name="mlp_down")

    h = rmsnorm(x, norm_ple, BF16)
    pp = proj_norm(p, w_ple_proj, ple_post_norm, BF16)
    x = mm(h, w_ple_gate, tm=512, tn=1024, out_dtype=F32, epilogue="ple",
           extra=((x, 0), (pp, 0)), name="ple_gate")
    return x


def kernel(x, p, w_in, conv_w, dn_a_log, dn_dt_bias, dn_norm, w_attn_up, w_dn_up, w_out,
           w_mlp_up, w_mlp_down, w_ple_gate, w_ple_proj, norm_mix, norm_mlp, norm_ple,
           ple_post_norm, final_norm):
    b, s_len, d_model = x.shape
    assert b == 1
    depth = w_in.shape[0]
    xs = x.reshape(s_len, d_model)
    for i in range(depth):
        xs = _layer(xs, p[i, 0], w_in[i], conv_w[i], dn_a_log[i], dn_dt_bias[i], dn_norm[i],
                    w_attn_up[i], w_dn_up[i], w_out[i], w_mlp_up[i], w_mlp_down[i],
                    w_ple_gate[i], w_ple_proj[i], norm_mix[i], norm_mlp[i], norm_ple[i],
                    ple_post_norm[i])
    out = rmsnorm(xs, final_norm, x.dtype)
    return out.reshape(b, s_len, d_model)
```

```python
import functools
import math

import jax
import jax.numpy as jnp
from jax import lax
from jax.experimental import pallas as pl
from jax.experimental.pallas import tpu as pltpu

F32 = jnp.float32
BF16 = jnp.bfloat16

EPS = 1e-6
LANES = 128
SUBLANES = 8
MXU_WIDTH = 256
ROW_SUB = 256
HEAD_DIM = 128
ATTN_SPAN = 128
ATTN_DILATIONS = (1, 4, 16)
ATTN_HEADS = 8
DN_HEADS = 16
DN_CHUNK = 64
CONV_K = 4

VMEM_LIMIT_BYTES = 56 * 1024 * 1024


def _params(n_axes):
    return pltpu.CompilerParams(
        dimension_semantics=("arbitrary",) * n_axes,
        vmem_limit_bytes=VMEM_LIMIT_BYTES,
    )


def _rmsnorm_kernel(x_ref, g_ref, o_ref):
    x = x_ref[...].astype(F32)
    ms = jnp.mean(x * x, axis=-1, keepdims=True)
    o_ref[...] = (x * lax.rsqrt(ms + EPS) * g_ref[...]).astype(o_ref.dtype)


def rmsnorm(x, g, out_dtype, *, tm=512):
    m, d = x.shape
    return pl.pallas_call(
        _rmsnorm_kernel,
        grid=(m // tm,),
        in_specs=[pl.BlockSpec((tm, d), lambda i: (i, 0)),
                  pl.BlockSpec((1, d), lambda i: (0, 0))],
        out_specs=pl.BlockSpec((tm, d), lambda i: (i, 0)),
        out_shape=jax.ShapeDtypeStruct((m, d), out_dtype),
        compiler_params=_params(1),
        name="rmsnorm",
    )(x, g.reshape(1, d).astype(F32))


def _sigmoid(x):
    return 1.0 / (1.0 + jnp.exp(-x))


def _mm_kernel(*refs, nk, tm, epilogue, n_extra, b_is_nk):
    a_ref, bchunk_ref = refs[0], refs[1]
    extra = refs[2:2 + n_extra]
    o_ref = refs[2 + n_extra]
    w_even, w_odd = refs[3 + n_extra], refs[4 + n_extra]
    acc_ref = refs[5 + n_extra]
    t = pl.program_id(0)
    i = pl.program_id(1)
    chunk_rows = bchunk_ref.shape[0]

    def stage(w_ref):
        rows = pl.ds(pl.multiple_of(i * chunk_rows, chunk_rows), chunk_rows)
        w_ref[rows, :] = bchunk_ref[...].astype(BF16)

    def compute(w_ref):
        _mm_compute(a_ref, w_ref, extra, o_ref, acc_ref, t - 1, i, nk=nk, tm=tm,
                    epilogue=epilogue, b_is_nk=b_is_nk)

    @pl.when(t == 0)
    def _():
        stage(w_even)

    @pl.when(jnp.logical_and(t > 0, t % 2 == 1))
    def _():
        stage(w_odd)
        compute(w_even)

    @pl.when(jnp.logical_and(t > 0, t % 2 == 0))
    def _():
        stage(w_even)
        compute(w_odd)


def _mm_compute(a_ref, w_ref, extra, o_ref, acc_ref, tile, i, *, nk, tm, epilogue, b_is_nk):
    k = tile % nk
    tn = o_ref.shape[-1]
    strip = min(MXU_WIDTH, tn)
    conv = epilogue.startswith("conv_silu")

    def product(row_slice, cols):
        if b_is_nk:
            return lax.dot_general(a_ref[row_slice, :], w_ref[cols, :], (((1,), (1,)), ((), ())),
                                   preferred_element_type=F32)
        return jnp.dot(a_ref[row_slice, :], w_ref[:, cols], preferred_element_type=F32)

    if not conv and epilogue != "plain_by_residue":
        row_sub = min(ROW_SUB, tm)
        for rs in range(tm // row_sub):
            rr = slice(rs * row_sub, (rs + 1) * row_sub)
            for c in range(tn // strip):
                cols = slice(c * strip, (c + 1) * strip)
                acc = product(rr, cols)
                if nk > 1:
                    arows = pl.ds(pl.multiple_of(i * tm + rs * row_sub, row_sub), row_sub)
                    acc = acc + jnp.where(k == 0, 0.0, acc_ref[arows, cols])
                    acc_ref[arows, cols] = acc
                if epilogue == "plain":
                    val = acc
                elif epilogue == "relu2":
                    r = jnp.maximum(acc, 0.0)
                    val = r * r
                elif epilogue == "sigmoid":
                    val = _sigmoid(acc)
                elif epilogue == "residual":
                    val = extra[0][rr, cols].astype(F32) + acc
                elif epilogue == "mul":
                    val = extra[0][rr, cols].astype(F32) * acc
                elif epilogue == "mul_add":
                    val = extra[1][rr, cols].astype(F32) + extra[0][rr, cols].astype(F32) * acc
                elif epilogue == "ple":
                    val = (extra[0][rr, cols].astype(F32)
                           + _sigmoid(acc) * extra[1][rr, cols].astype(F32))
                else:
                    raise ValueError(epilogue)
                o_ref[rr, cols] = val.astype(o_ref.dtype)
        return

    assert nk == 1
    all_rows = slice(0, tm)
    if epilogue == "plain_by_residue":
        for c in range(tn // strip):
            acc = product(all_rows, slice(c * strip, (c + 1) * strip))
            for piece in range(strip // LANES):
                acc_ref[c * (strip // LANES) + piece] = acc[:, piece * LANES:(piece + 1) * LANES]
        n_res = o_ref.shape[0]
        for piece in range(tn // LANES):
            for r in range(n_res):
                o_ref[r, :, piece * LANES:(piece + 1) * LANES] = acc_ref.at[piece][
                    pl.ds(r, tm // n_res, stride=n_res), :].astype(o_ref.dtype)
        return

    halo = SUBLANES

    @pl.when(i == 0)
    def _():
        acc_ref[0:halo, :] = jnp.zeros((halo, tn), F32)

    for c in range(tn // strip):
        cols = slice(c * strip, (c + 1) * strip)
        acc_ref[halo:halo + tm, cols] = product(all_rows, cols)
    for c in range(tn // strip):
        cols = slice(c * strip, (c + 1) * strip)
        acc = acc_ref[halo:halo + tm, cols]
        cw = extra[0][:, cols]
        base = halo - (CONV_K - 1)
        y = acc * cw[CONV_K - 1:CONV_K]
        for tap in range(CONV_K - 1):
            y = y + acc_ref[base + tap:base + tap + tm, cols] * cw[tap:tap + 1]
        acc_ref[0:halo, cols] = acc[tm - halo:, :]
        y = y * _sigmoid(y)
        if epilogue == "conv_silu_l2norm":
            segs = []
            for hh in range(strip // HEAD_DIM):
                seg = y[:, hh * HEAD_DIM:(hh + 1) * HEAD_DIM]
                segs.append(seg * lax.rsqrt(jnp.sum(seg * seg, axis=-1, keepdims=True) + EPS))
            y = jnp.concatenate(segs, axis=1)
        o_ref[:, cols] = y.astype(o_ref.dtype)


def matmul_ws(a, b, *, col0=0, n=None, out_dtype=BF16, epilogue="plain", extra=(),
              tm=1024, tn=512, tk=None, b_is_nk=False, panel_col=None, residues=1,
              name="matmul_ws"):
    m, kdim = a.shape
    n_total = b.shape[0] if b_is_nk else b.shape[1]
    if n is None:
        n = n_total - col0
    if tk is None:
        tk = kdim
    assert m % tm == 0 and n % tn == 0 and kdim % tk == 0 and col0 + n <= n_total
    assert panel_col is None or b_is_nk
    nj, nk, ni = n // tn, kdim // tk, m // tm
    n_tiles = nj * nk

    def tile_jk(t):
        tile = jnp.maximum(t - 1, 0)
        return tile // nk, tile % nk

    def row_block(t, i):
        _, k = tile_jk(t)
        return jnp.where(jnp.logical_and(t > 0, k == nk - 1), i, 0)

    def a_map(t, i):
        return (jnp.where(t > 0, i, 0), tile_jk(t)[1])

    def out_map(t, i):
        return (row_block(t, i), tile_jk(t)[0])

    def staged(t, i):
        tile = jnp.minimum(t, n_tiles - 1)
        return tile // nk, tile % nk, jnp.where(t == n_tiles, ni - 1, i)

    if b_is_nk:
        assert col0 % SUBLANES == 0 and tn % (ni * SUBLANES) == 0
        cn = tn // ni

        def b_map(t, i):
            j, k, c = staged(t, i)
            first = col0 + j * tn if panel_col is None else panel_col(j)
            return (pl.multiple_of(first + c * cn, SUBLANES), pl.multiple_of(k * tk, LANES))

        b_spec = pl.BlockSpec((pl.Element(cn), pl.Element(tk)), b_map)
        bcast_shape = (tn, tk)
    else:
        assert col0 % tn == 0 and tk % (ni * SUBLANES) == 0

        def b_map(t, i):
            j, k, c = staged(t, i)
            return (k * ni + c, j + col0 // tn)

        b_spec = pl.BlockSpec((tk // ni, tn), b_map)
        bcast_shape = (tk, tn)
    in_specs = [pl.BlockSpec((tm, tk), a_map), b_spec]
    conv = epilogue.startswith("conv_silu")
    for idx, (_, ecol0) in enumerate(extra):
        assert ecol0 % tn == 0
        if conv and idx == 0:
            in_specs.append(pl.BlockSpec(
                (CONV_K, tn),
                functools.partial(lambda t, i, e0: (0, tile_jk(t)[0] + e0), e0=ecol0 // tn)))
        else:
            in_specs.append(pl.BlockSpec(
                (tm, tn),
                functools.partial(lambda t, i, e0: (row_block(t, i), tile_jk(t)[0] + e0),
                                  e0=ecol0 // tn)))
    if epilogue == "plain_by_residue":
        assert nk == 1 and tm % (residues * SUBLANES) == 0
        out_spec = pl.BlockSpec((residues, tm // residues, tn),
                                lambda t, i: (0, row_block(t, i), tile_jk(t)[0]))
        out_shape = jax.ShapeDtypeStruct((residues, m // residues, n), out_dtype)
    else:
        out_spec = pl.BlockSpec((tm, tn), out_map)
        out_shape = jax.ShapeDtypeStruct((m, n), out_dtype)
    scratch = [pltpu.VMEM(bcast_shape, BF16), pltpu.VMEM(bcast_shape, BF16)]
    if conv:
        assert nk == 1
        scratch.append(pltpu.VMEM((SUBLANES + tm, tn), F32))
    elif epilogue == "plain_by_residue":
        scratch.append(pltpu.VMEM((tn // LANES, tm, LANES), F32))
    else:
        scratch.append(pltpu.VMEM((m, tn) if nk > 1 else (SUBLANES, LANES), F32))
    return pl.pallas_call(
        functools.partial(_mm_kernel, nk=nk, tm=tm, epilogue=epilogue, n_extra=len(extra),
                          b_is_nk=b_is_nk),
        grid=(n_tiles + 1, ni),
        in_specs=in_specs,
        out_specs=out_spec,
        out_shape=out_shape,
        scratch_shapes=scratch,
        compiler_params=_params(2),
        name=name,
    )(a, b, *[arr for arr, _ in extra])


def _attn_kernel(q_ref, kp_ref, kc_ref, vp_ref, vc_ref, o_ref, lse_ref, *, heads, scale):
    n = pl.program_id(1)
    planes, sub, _ = q_ref.shape
    span = planes * sub
    rho = lax.broadcasted_iota(jnp.int32, (span, 2 * span), 0)
    kap = lax.broadcasted_iota(jnp.int32, (span, 2 * span), 1)
    in_cur = kap // span
    dist = (planes * (sub * (1 - in_cur) + rho % sub - kap % sub)
            + rho // sub - (kap % span) // sub)
    valid = (dist >= 0) & (dist <= span) & ((in_cur == 1) | (n > 0))
    lane = lax.broadcasted_iota(jnp.int32, (span, HEAD_DIM), 1)
    lse_tile = jnp.zeros((span, HEAD_DIM), F32)

    def tile(ref, cols):
        return jnp.concatenate([ref[a, :, cols] for a in range(planes)], axis=0)

    for h in range(heads):
        cols = slice(h * HEAD_DIM, (h + 1) * HEAD_DIM)
        q = tile(q_ref, cols)
        k = jnp.concatenate([tile(kp_ref, cols), tile(kc_ref, cols)], axis=0)
        v = jnp.concatenate([tile(vp_ref, cols), tile(vc_ref, cols)], axis=0)
        s = lax.dot_general(q, k, (((1,), (1,)), ((), ())), preferred_element_type=F32) * scale
        s = jnp.where(valid, s, -jnp.inf)
        m = jnp.max(s, axis=-1, keepdims=True)
        e = jnp.exp(s - m)
        l = jnp.sum(e, axis=-1, keepdims=True)
        o = (jnp.dot(e.astype(BF16), v, preferred_element_type=F32) / l).astype(o_ref.dtype)
        for a in range(planes):
            o_ref[a, :, cols] = o[a * sub:(a + 1) * sub]
        lse_tile = jnp.where(lane == h, m + jnp.log(l), lse_tile)
    for a in range(planes):
        lse_ref[a] = lse_tile[a * sub:(a + 1) * sub]


def banded_attention(q, k, v, *, heads=ATTN_HEADS, col_blocks=(0, 0, 0)):
    planes, n_sub, rows, _ = q.shape
    width = heads * HEAD_DIM
    sub = ATTN_SPAN // planes
    nb = rows // sub

    def cur_spec(cb):
        return pl.BlockSpec((planes, None, sub, width), lambda r, n: (0, r, n, cb))

    def prev_spec(cb):
        return pl.BlockSpec((planes, None, sub, width),
                            lambda r, n: (0, r, jnp.maximum(n - 1, 0), cb))

    qc, kc, vc = col_blocks
    return pl.pallas_call(
        functools.partial(_attn_kernel, heads=heads, scale=HEAD_DIM ** -0.5),
        grid=(n_sub, nb),
        in_specs=[cur_spec(qc), prev_spec(kc), cur_spec(kc), prev_spec(vc), cur_spec(vc)],
        out_specs=[pl.BlockSpec((planes, None, sub, width), lambda r, n: (0, r, n, 0)),
                   pl.BlockSpec((planes, None, sub, HEAD_DIM), lambda r, n: (0, r, n, 0))],
        out_shape=[jax.ShapeDtypeStruct((planes, n_sub, rows, width), BF16),
                   jax.ShapeDtypeStruct((planes, n_sub, rows, HEAD_DIM), F32)],
        compiler_params=_params(2),
        name="banded_attention",
    )(q, k, k, v, v)


def _attn_combine_kernel(o0_ref, o1_ref, o2_ref, l0_ref, l1_ref, l2_ref, out_ref, *, heads):
    l0, l1, l2 = l0_ref[...], l1_ref[...], l2_ref[...]
    m = jnp.maximum(jnp.maximum(l0, l1), l2)
    e0, e1, e2 = jnp.exp(l0 - m), jnp.exp(l1 - m), jnp.exp(l2 - m)
    den = e0 + e1 + e2
    w0, w1, w2 = e0 / den, e1 / den, e2 / den
    for h in range(heads):
        cols = slice(h * HEAD_DIM, (h + 1) * HEAD_DIM)
        acc = (w0[:, h:h + 1] * o0_ref[:, cols] + w1[:, h:h + 1] * o1_ref[:, cols]
               + w2[:, h:h + 1] * o2_ref[:, cols])
        out_ref[:, cols] = acc.astype(out_ref.dtype)


def attn_combine(outs, lses, *, tm=256):
    m, width = outs[0].shape
    heads = width // HEAD_DIM
    ospec = pl.BlockSpec((tm, width), lambda i: (i, 0))
    lspec = pl.BlockSpec((tm, HEAD_DIM), lambda i: (i, 0))
    return pl.pallas_call(
        functools.partial(_attn_combine_kernel, heads=heads),
        grid=(m // tm,),
        in_specs=[ospec] * 3 + [lspec] * 3,
        out_specs=ospec,
        out_shape=jax.ShapeDtypeStruct((m, width), BF16),
        compiler_params=_params(1),
        name="attn_combine",
    )(*outs, *lses)


def _softplus(x):
    return jnp.maximum(x, 0.0) + jnp.log(1.0 + jnp.exp(-jnp.abs(x)))


def _bdot(a, b, dims):
    return lax.dot_general(a, b, (dims, ((0,), (0,))), preferred_element_type=F32)


def _bmm(a, b):
    return _bdot(a, b, ((2,), (1,)))


def _split_bf16(x):
    hi = x.astype(BF16)
    lo = (x - hi.astype(F32)).astype(BF16)
    return hi, lo


def _bmm_split(a, b):
    a_hi, a_lo = _split_bf16(a)
    b_hi, b_lo = _split_bf16(b)
    return _bmm(a_hi, b_hi) + _bmm(a_hi, b_lo) + _bmm(a_lo, b_hi)


def _unit_lower_inverse_batched(a):
    c = a.shape[-1]
    b0 = SUBLANES
    ri = lax.broadcasted_iota(jnp.int32, (c, c), 0)
    ci = lax.broadcasted_iota(jnp.int32, (c, c), 1)
    below = ri > ci
    eye = (ri == ci).astype(F32)

    def same_block(b):
        return (ri // b) == (ci // b)

    def mm(x, y):
        return _bmm(x.astype(BF16), y.astype(BF16))

    d = jnp.where(same_block(b0) & below, a, 0.0)
    p = eye - d
    x = d
    for _ in range(int(math.log2(b0)) - 1):
        x = mm(x, x)
        p = p + mm(p, x)
    b = b0
    while b < c:
        off_diag = jnp.where(same_block(2 * b) & jnp.logical_not(same_block(b)) & below, a, 0.0)
        p = p - mm(mm(p, off_diag), p)
        b *= 2
    resid = eye - p - _bmm_split(a, p)
    return p + mm(p, resid)


def _dn_prep_kernel(q_ref, k_ref, v_ref, ba_ref, bat_ref, prow_ref, pcol_ref,
                    u_ref, w_ref, qg_ref, kd_ref, intra_ref, eg_ref, *, heads, chunks):
    c_len = DN_CHUNK

    ba = ba_ref[...]
    beta_col = _sigmoid(ba)
    g_col = -jnp.exp(prow_ref[0:1, :]) * _softplus(ba + prow_ref[1:2, :])
    row_in_chunk = lax.broadcasted_iota(jnp.int32, g_col.shape, 0) % c_len
    shift = 1
    while shift < c_len:
        g_col = g_col + jnp.where(row_in_chunk >= shift, pltpu.roll(g_col, shift, 0), 0.0)
        shift *= 2
    bat = bat_ref[...]
    beta_row = _sigmoid(bat)
    g_row = -jnp.exp(pcol_ref[:, 0:1])[None] * _softplus(bat + pcol_ref[:, 1:2][None])
    lane_in_chunk = lax.broadcasted_iota(jnp.int32, g_row.shape, 2)
    shift = 1
    while shift < c_len:
        g_row = g_row + jnp.where(lane_in_chunk >= shift, pltpu.roll(g_row, shift, 2), 0.0)
        shift *= 2
    eg_ref[...] = jnp.exp(g_row)

    ri = lax.broadcasted_iota(jnp.int32, (c_len, c_len), 0)
    ci = lax.broadcasted_iota(jnp.int32, (c_len, c_len), 1)
    lower = ri >= ci
    strict = ri > ci

    units = [(c, h) for c in range(chunks) for h in range(heads)]

    def rows_of(c):
        return slice(c * c_len, (c + 1) * c_len)

    def per_unit(x_ref):
        return jnp.stack([x_ref[rows_of(c), h * HEAD_DIM:(h + 1) * HEAD_DIM] for c, h in units])

    q = per_unit(q_ref).astype(F32) * HEAD_DIM ** -0.5
    k16 = per_unit(k_ref)
    k = k16.astype(F32)
    v16 = per_unit(v_ref)
    gcol = jnp.stack([g_col[rows_of(c), heads + h:heads + h + 1] for c, h in units])
    bcol = jnp.stack([beta_col[rows_of(c), h:h + 1] for c, h in units])
    grow = jnp.stack([g_row[c, heads + h:heads + h + 1, :] for c, h in units])
    brow = jnp.stack([beta_row[c, h:h + 1, :] for c, h in units])
    decay = jnp.exp(jnp.where(lower, gcol - grow, -jnp.inf))
    q16 = q.astype(BF16)
    qk = _bdot(jnp.concatenate([q16, k16], axis=1), k16, ((2,), (2,)))
    intra = qk[:, :c_len] * decay
    a = jnp.where(strict, qk[:, c_len:] * decay * bcol, 0.0)
    t_inv = _unit_lower_inverse_batched(a)
    u = _bmm((t_inv * brow).astype(BF16), v16)
    w = _bmm((t_inv * (brow * jnp.exp(grow))).astype(BF16), k16)
    g_last = grow[:, :, c_len - 1:c_len]
    qg = q * jnp.exp(gcol)
    kd = k * jnp.exp(g_last - gcol)
    for c in range(chunks):
        rs, sl = rows_of(c), slice(c * heads, (c + 1) * heads)
        u_ref[:, rs, :] = u[sl]
        w_ref[:, rs, :] = w[sl].astype(w_ref.dtype)
        qg_ref[:, rs, :] = qg[sl].astype(qg_ref.dtype)
        kd_ref[:, rs, :] = kd[sl].astype(kd_ref.dtype)
        intra_ref[:, rs, :] = intra[sl].astype(intra_ref.dtype)


def _dn_scan_kernel(egl_ref, u_ref, w_ref, qg_ref, kd_ref, intra_ref, z_ref, gn_ref, o_ref,
                    state_ref, *, heads, chunks):
    tb = pl.program_id(0)
    c_len = DN_CHUNK

    @pl.when(tb == 0)
    def _():
        state_ref[...] = jnp.zeros_like(state_ref)

    gn = gn_ref[...]
    for c in range(chunks):
        rs = slice(c * c_len, (c + 1) * c_len)
        s = state_ref[...]
        lhs = jnp.concatenate([w_ref[:, rs, :], qg_ref[:, rs, :]], axis=1)
        r = _bmm(lhs, s.astype(BF16))
        v_new = u_ref[:, rs, :] - r[:, :c_len]
        vn16 = v_new.astype(BF16)
        o = r[:, c_len:] + _bmm(intra_ref[:, rs, :], vn16)
        upd = _bdot(kd_ref[:, rs, :], vn16, ((1,), (1,)))
        base = (tb * chunks + c) * heads
        for h in range(heads):
            state_ref[h] = s[h] * egl_ref[base + h] + upd[h]
        o = o * lax.rsqrt(jnp.mean(o * o, axis=-1, keepdims=True) + EPS) * gn
        for h in range(heads):
            cs = slice(h * HEAD_DIM, (h + 1) * HEAD_DIM)
            zz = z_ref[rs, cs].astype(F32)
            o_ref[rs, cs] = (o[h] * (zz * _sigmoid(zz))).astype(o_ref.dtype)


def gated_deltanet2(qk, v, ba, a_log, dt_bias, z, dn_norm, *, prep_chunks=4, scan_chunks=8):
    s_len, width = v.shape
    heads = width // HEAD_DIM
    c_len = DN_CHUNK
    n_chunks = s_len // c_len
    bat = ba[:, :2 * heads].reshape(n_chunks, c_len, 2 * heads).transpose(0, 2, 1)
    prow = jnp.zeros((2, HEAD_DIM), F32)
    prow = prow.at[0, heads:2 * heads].set(a_log.astype(F32))
    prow = prow.at[1, heads:2 * heads].set(dt_bias.astype(F32))
    pcol = jnp.zeros((2 * heads, 2), F32)
    pcol = pcol.at[heads:, 0].set(a_log.astype(F32)).at[heads:, 1].set(dt_bias.astype(F32))

    rows = prep_chunks * c_len

    def main_spec(i):
        return pl.BlockSpec((rows, width), lambda t: (t, i))

    def prep_out(last):
        return pl.BlockSpec((heads, rows, last), lambda t: (0, t, 0))

    u, w, qg, kd, intra, eg = pl.pallas_call(
        functools.partial(_dn_prep_kernel, heads=heads, chunks=prep_chunks),
        grid=(s_len // rows,),
        in_specs=[main_spec(0), main_spec(1), main_spec(0),
                  pl.BlockSpec((rows, HEAD_DIM), lambda t: (t, 0)),
                  pl.BlockSpec((prep_chunks, 2 * heads, c_len), lambda t: (t, 0, 0)),
                  pl.BlockSpec((2, HEAD_DIM), lambda t: (0, 0)),
                  pl.BlockSpec((2 * heads, 2), lambda t: (0, 0))],
        out_specs=[prep_out(HEAD_DIM), prep_out(HEAD_DIM), prep_out(HEAD_DIM),
                   prep_out(HEAD_DIM), prep_out(c_len),
                   pl.BlockSpec((prep_chunks, 2 * heads, c_len), lambda t: (t, 0, 0))],
        out_shape=[jax.ShapeDtypeStruct((heads, s_len, HEAD_DIM), F32),
                   jax.ShapeDtypeStruct((heads, s_len, HEAD_DIM), BF16),
                   jax.ShapeDtypeStruct((heads, s_len, HEAD_DIM), BF16),
                   jax.ShapeDtypeStruct((heads, s_len, HEAD_DIM), BF16),
                   jax.ShapeDtypeStruct((heads, s_len, c_len), BF16),
                   jax.ShapeDtypeStruct((n_chunks, 2 * heads, c_len), F32)],
        compiler_params=_params(1),
        name="deltanet_prep",
    )(qk, qk, v, ba, bat, prow, pcol)

    egl = eg[:, heads:, c_len - 1].reshape(n_chunks * heads)

    scan_rows = scan_chunks * c_len

    def head_major(last):
        return pl.BlockSpec((heads, scan_rows, last), lambda t: (0, t, 0))

    nat = pl.BlockSpec((scan_rows, width), lambda t: (t, 0))
    return pl.pallas_call(
        functools.partial(_dn_scan_kernel, heads=heads, chunks=scan_chunks),
        grid=(s_len // scan_rows,),
        in_specs=[pl.BlockSpec(memory_space=pltpu.SMEM),
                  head_major(HEAD_DIM), head_major(HEAD_DIM), head_major(HEAD_DIM),
                  head_major(HEAD_DIM), head_major(c_len), nat,
                  pl.BlockSpec((1, HEAD_DIM), lambda t: (0, 0))],
        out_specs=nat,
        out_shape=jax.ShapeDtypeStruct((s_len, width), BF16),
        scratch_shapes=[pltpu.VMEM((heads, HEAD_DIM, HEAD_DIM), F32)],
        compiler_params=_params(1),
        name="deltanet_scan",
    )(egl, u, w, qg, kd, intra, z, dn_norm.reshape(1, HEAD_DIM).astype(F32))


def _proj_norm_kernel(p_ref, w_ref, g_ref, o_ref, wcast_ref):
    @pl.when(pl.program_id(0) == 0)
    def _():
        wcast_ref[...] = w_ref[...].astype(BF16)

    y = jnp.dot(p_ref[...].astype(BF16), wcast_ref[...], preferred_element_type=F32)
    ms = jnp.mean(y * y, axis=-1, keepdims=True)
    o_ref[...] = (y * lax.rsqrt(ms + EPS) * g_ref[...]).astype(o_ref.dtype)


def proj_norm(p, w, g, out_dtype, *, tm=512):
    m, kdim = p.shape
    n = w.shape[1]
    return pl.pallas_call(
        _proj_norm_kernel,
        grid=(m // tm,),
        in_specs=[pl.BlockSpec((tm, kdim), lambda i: (i, 0)),
                  pl.BlockSpec((kdim, n), lambda i: (0, 0)),
                  pl.BlockSpec((1, n), lambda i: (0, 0))],
        out_specs=pl.BlockSpec((tm, n), lambda i: (i, 0)),
        out_shape=jax.ShapeDtypeStruct((m, n), out_dtype),
        scratch_shapes=[pltpu.VMEM((kdim, n), BF16)],
        compiler_params=_params(1),
        name="proj_norm",
    )(p, w, g.reshape(1, n).astype(F32))


def _layer(x, p, w_in, conv_w, a_log, dt_bias, dn_norm, w_attn_up, w_dn_up, w_out, w_mlp_up,
           w_mlp_down, w_ple_gate, w_ple_proj, norm_mix, norm_mlp, norm_ple, ple_post_norm):
    s_len, d_model = x.shape
    attn_w = ATTN_HEADS * HEAD_DIM
    dn_w = DN_HEADS * HEAD_DIM
    n_groups = len(ATTN_DILATIONS)
    c_qkv_b = 3 * n_groups * attn_w
    c_z = c_qkv_b + 3 * dn_w
    c_ba = c_z + dn_w
    c_gate = c_ba + 2 * DN_HEADS
    mm = matmul_ws

    w_in_t = w_in.T
    h = rmsnorm(x, norm_mix, BF16)
    max_dil = max(ATTN_DILATIONS)
    z_panels = dn_w // attn_w
    z_qkv_g0 = mm(h, w_in_t, b_is_nk=True, n=dn_w + 3 * attn_w, tn=attn_w, out_dtype=BF16,
                  panel_col=lambda j: jnp.where(j < z_panels, c_z + j * attn_w,
                                                (j - z_panels) * (n_groups * attn_w)),
                  name="proj_z_attn_g0")
    qkv_g12 = mm(h, w_in_t, b_is_nk=True, n=6 * attn_w, tn=attn_w, out_dtype=BF16,
                 epilogue="plain_by_residue", residues=max_dil,
                 panel_col=lambda j: (j // 2) * (n_groups * attn_w) + (1 + j % 2) * attn_w,
                 name="proj_attn_g12")
    qk_b = mm(h, w_in_t, b_is_nk=True, col0=c_qkv_b, n=2 * dn_w, tn=1024, out_dtype=BF16,
              epilogue="conv_silu_l2norm", extra=((conv_w, 0),), name="proj_dn_qk")
    v_b = mm(h, w_in_t, b_is_nk=True, col0=c_qkv_b + 2 * dn_w, n=dn_w, tn=1024, out_dtype=BF16,
             epilogue="conv_silu", extra=((conv_w, 2 * dn_w),), name="proj_dn_v")
    ba = mm(h, w_in_t, b_is_nk=True, col0=c_ba, n=LANES, tm=2048, tn=LANES, out_dtype=F32,
            name="proj_ba")
    gates = mm(h, w_in_t, b_is_nk=True, col0=c_gate, n=2 * d_model, tn=1024, out_dtype=BF16,
               epilogue="sigmoid", name="proj_gates")

    g0 = z_qkv_g0.reshape(1, 1, s_len, dn_w + 3 * attn_w)
    o0, lse0 = banded_attention(g0, g0, g0,
                                col_blocks=(z_panels, z_panels + 1, z_panels + 2))
    per = s_len // max_dil
    g1 = qkv_g12.reshape(4, 4, per, 6 * attn_w)
    o1, lse1 = banded_attention(g1, g1, g1, col_blocks=(0, 2, 4))
    g2 = qkv_g12.reshape(1, max_dil, per, 6 * attn_w)
    o2, lse2 = banded_attention(g2, g2, g2, col_blocks=(1, 3, 5))

    def natural(t):
        return t.transpose(2, 0, 1, 3).reshape(s_len, t.shape[-1])

    o_a = attn_combine([o0.reshape(s_len, attn_w), natural(o1), natural(o2)],
                       [lse0.reshape(s_len, HEAD_DIM), natural(lse1), natural(lse2)])

    o_b = gated_deltanet2(qk_b, v_b, ba, a_log, dt_bias, z_qkv_g0, dn_norm)

    t_a = mm(o_a, w_attn_up, tm=2048, tn=1024, out_dtype=BF16, epilogue="mul",
             extra=((gates, 0),), name="attn_up")
    merged = mm(o_b, w_dn_up, tm=2048, tn=1024, out_dtype=BF16, epilogue="mul_add",
                extra=((gates, d_model), (t_a, 0)), name="dn_up")
    x = mm(merged, w_out, tm=1024, tn=1024, out_dtype=F32, epilogue="residual", extra=((x, 0),),
           name="out_proj")

    h = rmsnorm(x, norm_mlp, BF16)
    u = mm(h, w_mlp_up, tn=1024, out_dtype=BF16, epilogue="relu2", name="mlp_up")
    x = mm(u, w_mlp_down, out_dtype=F32, epilogue="residual", extra=((x, 0),), tk=4096,
           name="mlp_down")

    h = rmsnorm(x, norm_ple, BF16)
    pp = proj_norm(p, w_ple_proj, ple_post_norm, BF16)
    x = mm(h, w_ple_gate, tm=512, tn=1024, out_dtype=F32, epilogue="ple",
           extra=((x, 0), (pp, 0)), name="ple_gate")
    return x


def kernel(x, p, w_in, conv_w, dn_a_log, dn_dt_bias, dn_norm, w_attn_up, w_dn_up, w_out,
           w_mlp_up, w_mlp_down, w_ple_gate, w_ple_proj, norm_mix, norm_mlp, norm_ple,
           ple_post_norm, final_norm):
    b, s_len, d_model = x.shape
    assert b == 1
    depth = w_in.shape[0]
    xs = x.reshape(s_len, d_model)
    for i in range(depth):
        xs = _layer(xs, p[i, 0], w_in[i], conv_w[i], dn_a_log[i], dn_dt_bias[i], dn_norm[i],
                    w_attn_up[i], w_dn_up[i], w_out[i], w_mlp_up[i], w_mlp_down[i],
                    w_ple_gate[i], w_ple_proj[i], norm_mix[i], norm_mlp[i], norm_ple[i],
                    ple_post_norm[i])
    out = rmsnorm(xs, final_norm, x.dtype)
    return out.reshape(b, s_len, d_model)
```

```python
import functools
import math

import jax
import jax.numpy as jnp
from jax import lax
from jax.experimental import pallas as pl
from jax.experimental.pallas import tpu as pltpu

F32 = jnp.float32
BF16 = jnp.bfloat16

EPS = 1e-6
LANES = 128
SUBLANES = 8
MXU_WIDTH = 256
ROW_SUB = 256
HEAD_DIM = 128
ATTN_SPAN = 128
ATTN_DILATIONS = (1, 4, 16)
ATTN_HEADS = 8
DN_HEADS = 16
DN_CHUNK = 64
CONV_K = 4

VMEM_LIMIT_BYTES = 56 * 1024 * 1024


def _params(n_axes):
    return pltpu.CompilerParams(
        dimension_semantics=("arbitrary",) * n_axes,
        vmem_limit_bytes=VMEM_LIMIT_BYTES,
    )


def _rmsnorm_kernel(x_ref, g_ref, o_ref):
    x = x_ref[...].astype(F32)
    ms = jnp.mean(x * x, axis=-1, keepdims=True)
    o_ref[...] = (x * lax.rsqrt(ms + EPS) * g_ref[...]).astype(o_ref.dtype)


def rmsnorm(x, g, out_dtype, *, tm=512):
    m, d = x.shape

    def body(x_hbm, g_ref, o_hbm):
        pltpu.emit_pipeline(
            lambda x_ref, o_ref: _rmsnorm_kernel(x_ref, g_ref, o_ref),
            grid=(m // tm,),
            in_specs=[pl.BlockSpec((tm, d), lambda i: (i, 0), pipeline_mode=pl.Buffered(3))],
            out_specs=[pl.BlockSpec((tm, d), lambda i: (i, 0))],
        )(x_hbm, o_hbm)

    return pl.pallas_call(
        body,
        in_specs=[pl.BlockSpec(memory_space=pl.ANY),
                  pl.BlockSpec(memory_space=pltpu.VMEM)],
        out_specs=pl.BlockSpec(memory_space=pl.ANY),
        out_shape=jax.ShapeDtypeStruct((m, d), out_dtype),
        compiler_params=pltpu.CompilerParams(vmem_limit_bytes=VMEM_LIMIT_BYTES),
        name="rmsnorm",
    )(x, g.reshape(1, d).astype(F32))


def _sigmoid(x):
    return 1.0 / (1.0 + jnp.exp(-x))


def _mm_kernel(*refs, nk, tm, epilogue, n_extra, b_is_nk):
    a_ref, bchunk_ref = refs[0], refs[1]
    extra = refs[2:2 + n_extra]
    o_ref = refs[2 + n_extra]
    w_even, w_odd = refs[3 + n_extra], refs[4 + n_extra]
    acc_ref = refs[5 + n_extra]
    t = pl.program_id(0)
    i = pl.program_id(1)
    chunk_rows = bchunk_ref.shape[0]

    def stage(w_ref):
        rows = pl.ds(pl.multiple_of(i * chunk_rows, chunk_rows), chunk_rows)
        w_ref[rows, :] = bchunk_ref[...].astype(BF16)

    def compute(w_ref):
        _mm_compute(a_ref, w_ref, extra, o_ref, acc_ref, t - 1, i, nk=nk, tm=tm,
                    epilogue=epilogue, b_is_nk=b_is_nk)

    @pl.when(t == 0)
    def _():
        stage(w_even)

    @pl.when(jnp.logical_and(t > 0, t % 2 == 1))
    def _():
        stage(w_odd)
        compute(w_even)

    @pl.when(jnp.logical_and(t > 0, t % 2 == 0))
    def _():
        stage(w_even)
        compute(w_odd)


def _mm_compute(a_ref, w_ref, extra, o_ref, acc_ref, tile, i, *, nk, tm, epilogue, b_is_nk):
    k = tile % nk
    tn = o_ref.shape[-1]
    strip = min(MXU_WIDTH, tn)
    conv = epilogue.startswith("conv_silu")

    def product(row_slice, cols):
        if b_is_nk:
            return lax.dot_general(a_ref[row_slice, :], w_ref[cols, :], (((1,), (1,)), ((), ())),
                                   preferred_element_type=F32)
        return jnp.dot(a_ref[row_slice, :], w_ref[:, cols], preferred_element_type=F32)

    if not conv and epilogue != "plain_by_residue":
        row_sub = min(ROW_SUB, tm)
        for rs in range(tm // row_sub):
            rr = slice(rs * row_sub, (rs + 1) * row_sub)
            for c in range(tn // strip):
                cols = slice(c * strip, (c + 1) * strip)
                acc = product(rr, cols)
                if nk > 1:
                    arows = pl.ds(pl.multiple_of(i * tm + rs * row_sub, row_sub), row_sub)
                    acc = acc + jnp.where(k == 0, 0.0, acc_ref[arows, cols])
                    acc_ref[arows, cols] = acc
                if epilogue == "plain":
                    val = acc
                elif epilogue == "relu2":
                    r = jnp.maximum(acc, 0.0)
                    val = r * r
                elif epilogue == "sigmoid":
                    val = _sigmoid(acc)
                elif epilogue == "residual":
                    val = extra[0][rr, cols].astype(F32) + acc
                elif epilogue == "mul":
                    val = extra[0][rr, cols].astype(F32) * acc
                elif epilogue == "mul_add":
                    val = extra[1][rr, cols].astype(F32) + extra[0][rr, cols].astype(F32) * acc
                elif epilogue == "ple":
                    val = (extra[0][rr, cols].astype(F32)
                           + _sigmoid(acc) * extra[1][rr, cols].astype(F32))
                else:
                    raise ValueError(epilogue)
                o_ref[rr, cols] = val.astype(o_ref.dtype)
        return

    assert nk == 1
    all_rows = slice(0, tm)
    if epilogue == "plain_by_residue":
        for c in range(tn // strip):
            acc = product(all_rows, slice(c * strip, (c + 1) * strip))
            for piece in range(strip // LANES):
                acc_ref[c * (strip // LANES) + piece] = acc[:, piece * LANES:(piece + 1) * LANES]
        n_res = o_ref.shape[0]
        for piece in range(tn // LANES):
            for r in range(n_res):
                o_ref[r, :, piece * LANES:(piece + 1) * LANES] = acc_ref.at[piece][
                    pl.ds(r, tm // n_res, stride=n_res), :].astype(o_ref.dtype)
        return

    halo = SUBLANES

    @pl.when(i == 0)
    def _():
        acc_ref[0:halo, :] = jnp.zeros((halo, tn), F32)

    for c in range(tn // strip):
        cols = slice(c * strip, (c + 1) * strip)
        acc_ref[halo:halo + tm, cols] = product(all_rows, cols)
    for c in range(tn // strip):
        cols = slice(c * strip, (c + 1) * strip)
        acc = acc_ref[halo:halo + tm, cols]
        cw = extra[0][:, cols]
        base = halo - (CONV_K - 1)
        y = acc * cw[CONV_K - 1:CONV_K]
        for tap in range(CONV_K - 1):
            y = y + acc_ref[base + tap:base + tap + tm, cols] * cw[tap:tap + 1]
        acc_ref[0:halo, cols] = acc[tm - halo:, :]
        y = y * _sigmoid(y)
        if epilogue == "conv_silu_l2norm":
            segs = []
            for hh in range(strip // HEAD_DIM):
                seg = y[:, hh * HEAD_DIM:(hh + 1) * HEAD_DIM]
                segs.append(seg * lax.rsqrt(jnp.sum(seg * seg, axis=-1, keepdims=True) + EPS))
            y = jnp.concatenate(segs, axis=1)
        o_ref[:, cols] = y.astype(o_ref.dtype)


def matmul_ws(a, b, *, col0=0, n=None, out_dtype=BF16, epilogue="plain", extra=(),
              tm=1024, tn=512, tk=None, b_is_nk=False, panel_col=None, residues=1,
              name="matmul_ws"):
    m, kdim = a.shape
    n_total = b.shape[0] if b_is_nk else b.shape[1]
    if n is None:
        n = n_total - col0
    if tk is None:
        tk = kdim
    assert m % tm == 0 and n % tn == 0 and kdim % tk == 0 and col0 + n <= n_total
    assert panel_col is None or b_is_nk
    nj, nk, ni = n // tn, kdim // tk, m // tm
    n_tiles = nj * nk

    def tile_jk(t):
        tile = jnp.maximum(t - 1, 0)
        return tile // nk, tile % nk

    def row_block(t, i):
        _, k = tile_jk(t)
        return jnp.where(jnp.logical_and(t > 0, k == nk - 1), i, 0)

    def a_map(t, i):
        return (jnp.where(t > 0, i, 0), tile_jk(t)[1])

    def out_map(t, i):
        return (row_block(t, i), tile_jk(t)[0])

    def staged(t, i):
        tile = jnp.minimum(t, n_tiles - 1)
        return tile // nk, tile % nk, jnp.where(t == n_tiles, ni - 1, i)

    if b_is_nk:
        assert col0 % SUBLANES == 0 and tn % (ni * SUBLANES) == 0
        cn = tn // ni

        def b_map(t, i):
            j, k, c = staged(t, i)
            first = col0 + j * tn if panel_col is None else panel_col(j)
            return (pl.multiple_of(first + c * cn, SUBLANES), pl.multiple_of(k * tk, LANES))

        b_spec = pl.BlockSpec((pl.Element(cn), pl.Element(tk)), b_map)
        bcast_shape = (tn, tk)
    else:
        assert col0 % tn == 0 and tk % (ni * SUBLANES) == 0

        def b_map(t, i):
            j, k, c = staged(t, i)
            return (k * ni + c, j + col0 // tn)

        b_spec = pl.BlockSpec((tk // ni, tn), b_map)
        bcast_shape = (tk, tn)
    in_specs = [pl.BlockSpec((tm, tk), a_map), b_spec]
    conv = epilogue.startswith("conv_silu")
    for idx, (_, ecol0) in enumerate(extra):
        assert ecol0 % tn == 0
        if conv and idx == 0:
            in_specs.append(pl.BlockSpec(
                (CONV_K, tn),
                functools.partial(lambda t, i, e0: (0, tile_jk(t)[0] + e0), e0=ecol0 // tn)))
        else:
            in_specs.append(pl.BlockSpec(
                (tm, tn),
                functools.partial(lambda t, i, e0: (row_block(t, i), tile_jk(t)[0] + e0),
                                  e0=ecol0 // tn)))
    if epilogue == "plain_by_residue":
        assert nk == 1 and tm % (residues * SUBLANES) == 0
        out_spec = pl.BlockSpec((residues, tm // residues, tn),
                                lambda t, i: (0, row_block(t, i), tile_jk(t)[0]))
        out_shape = jax.ShapeDtypeStruct((residues, m // residues, n), out_dtype)
    else:
        out_spec = pl.BlockSpec((tm, tn), out_map)
        out_shape = jax.ShapeDtypeStruct((m, n), out_dtype)
    scratch = [pltpu.VMEM(bcast_shape, BF16), pltpu.VMEM(bcast_shape, BF16)]
    if conv:
        assert nk == 1
        scratch.append(pltpu.VMEM((SUBLANES + tm, tn), F32))
    elif epilogue == "plain_by_residue":
        scratch.append(pltpu.VMEM((tn // LANES, tm, LANES), F32))
    else:
        scratch.append(pltpu.VMEM((m, tn) if nk > 1 else (SUBLANES, LANES), F32))
    return pl.pallas_call(
        functools.partial(_mm_kernel, nk=nk, tm=tm, epilogue=epilogue, n_extra=len(extra),
                          b_is_nk=b_is_nk),
        grid=(n_tiles + 1, ni),
        in_specs=in_specs,
        out_specs=out_spec,
        out_shape=out_shape,
        scratch_shapes=scratch,
        compiler_params=_params(2),
        name=name,
    )(a, b, *[arr for arr, _ in extra])


def _attn_kernel(q_ref, kp_ref, kc_ref, vp_ref, vc_ref, o_ref, lse_ref, *, heads, scale):
    n = pl.program_id(1)
    planes, sub, _ = q_ref.shape
    span = planes * sub
    rho = lax.broadcasted_iota(jnp.int32, (span, 2 * span), 0)
    kap = lax.broadcasted_iota(jnp.int32, (span, 2 * span), 1)
    in_cur = kap // span
    dist = (planes * (sub * (1 - in_cur) + rho % sub - kap % sub)
            + rho // sub - (kap % span) // sub)
    valid = (dist >= 0) & (dist <= span) & ((in_cur == 1) | (n > 0))
    lane = lax.broadcasted_iota(jnp.int32, (span, HEAD_DIM), 1)
    lse_tile = jnp.zeros((span, HEAD_DIM), F32)

    def tile(ref, cols):
        return jnp.concatenate([ref[a, :, cols] for a in range(planes)], axis=0)

    for h in range(heads):
        cols = slice(h * HEAD_DIM, (h + 1) * HEAD_DIM)
        q = tile(q_ref, cols)
        k = jnp.concatenate([tile(kp_ref, cols), tile(kc_ref, cols)], axis=0)
        v = jnp.concatenate([tile(vp_ref, cols), tile(vc_ref, cols)], axis=0)
        s = lax.dot_general(q, k, (((1,), (1,)), ((), ())), preferred_element_type=F32) * scale
        s = jnp.where(valid, s, -jnp.inf)
        m = jnp.max(s, axis=-1, keepdims=True)
        e = jnp.exp(s - m)
        l = jnp.sum(e, axis=-1, keepdims=True)
        o = (jnp.dot(e.astype(BF16), v, preferred_element_type=F32) / l).astype(o_ref.dtype)
        for a in range(planes):
            o_ref[a, :, cols] = o[a * sub:(a + 1) * sub]
        lse_tile = jnp.where(lane == h, m + jnp.log(l), lse_tile)
    for a in range(planes):
        lse_ref[a] = lse_tile[a * sub:(a + 1) * sub]


def banded_attention(q, k, v, *, heads=ATTN_HEADS, col_blocks=(0, 0, 0)):
    planes, n_sub, rows, _ = q.shape
    width = heads * HEAD_DIM
    sub = ATTN_SPAN // planes
    nb = rows // sub

    def cur_spec(cb):
        return pl.BlockSpec((planes, None, sub, width), lambda r, n: (0, r, n, cb))

    def prev_spec(cb):
        return pl.BlockSpec((planes, None, sub, width),
                            lambda r, n: (0, r, jnp.maximum(n - 1, 0), cb))

    qc, kc, vc = col_blocks
    return pl.pallas_call(
        functools.partial(_attn_kernel, heads=heads, scale=HEAD_DIM ** -0.5),
        grid=(n_sub, nb),
        in_specs=[cur_spec(qc), prev_spec(kc), cur_spec(kc), prev_spec(vc), cur_spec(vc)],
        out_specs=[pl.BlockSpec((planes, None, sub, width), lambda r, n: (0, r, n, 0)),
                   pl.BlockSpec((planes, None, sub, HEAD_DIM), lambda r, n: (0, r, n, 0))],
        out_shape=[jax.ShapeDtypeStruct((planes, n_sub, rows, width), BF16),
                   jax.ShapeDtypeStruct((planes, n_sub, rows, HEAD_DIM), F32)],
        compiler_params=_params(2),
        name="banded_attention",
    )(q, k, k, v, v)


def _attn_combine_kernel(o0_ref, o1_ref, o2_ref, l0_ref, l1_ref, l2_ref, out_ref, *, heads):
    l0, l1, l2 = l0_ref[...], l1_ref[...], l2_ref[...]
    m = jnp.maximum(jnp.maximum(l0, l1), l2)
    e0, e1, e2 = jnp.exp(l0 - m), jnp.exp(l1 - m), jnp.exp(l2 - m)
    den = e0 + e1 + e2
    w0, w1, w2 = e0 / den, e1 / den, e2 / den
    for h in range(heads):
        cols = slice(h * HEAD_DIM, (h + 1) * HEAD_DIM)
        acc = (w0[:, h:h + 1] * o0_ref[:, cols] + w1[:, h:h + 1] * o1_ref[:, cols]
               + w2[:, h:h + 1] * o2_ref[:, cols])
        out_ref[:, cols] = acc.astype(out_ref.dtype)


def attn_combine(outs, lses, *, tm=256):
    m, width = outs[0].shape
    heads = width // HEAD_DIM
    ospec = pl.BlockSpec((tm, width), lambda i: (i, 0))
    lspec = pl.BlockSpec((tm, HEAD_DIM), lambda i: (i, 0))
    return pl.pallas_call(
        functools.partial(_attn_combine_kernel, heads=heads),
        grid=(m // tm,),
        in_specs=[ospec] * 3 + [lspec] * 3,
        out_specs=ospec,
        out_shape=jax.ShapeDtypeStruct((m, width), BF16),
        compiler_params=_params(1),
        name="attn_combine",
    )(*outs, *lses)


def _softplus(x):
    return jnp.maximum(x, 0.0) + jnp.log(1.0 + jnp.exp(-jnp.abs(x)))


def _bdot(a, b, dims):
    return lax.dot_general(a, b, (dims, ((0,), (0,))), preferred_element_type=F32)


def _bmm(a, b):
    return _bdot(a, b, ((2,), (1,)))


def _split_bf16(x):
    hi = x.astype(BF16)
    lo = (x - hi.astype(F32)).astype(BF16)
    return hi, lo


def _bmm_split(a, b):
    a_hi, a_lo = _split_bf16(a)
    b_hi, b_lo = _split_bf16(b)
    return _bmm(a_hi, b_hi) + _bmm(a_hi, b_lo) + _bmm(a_lo, b_hi)


def _unit_lower_inverse_batched(a):
    c = a.shape[-1]
    b0 = SUBLANES
    ri = lax.broadcasted_iota(jnp.int32, (c, c), 0)
    ci = lax.broadcasted_iota(jnp.int32, (c, c), 1)
    below = ri > ci
    eye = (ri == ci).astype(F32)

    def same_block(b):
        return (ri // b) == (ci // b)

    def mm(x, y):
        return _bmm(x.astype(BF16), y.astype(BF16))

    d = jnp.where(same_block(b0) & below, a, 0.0)
    p = eye - d
    x = d
    for _ in range(int(math.log2(b0)) - 1):
        x = mm(x, x)
        p = p + mm(p, x)
    b = b0
    while b < c:
        off_diag = jnp.where(same_block(2 * b) & jnp.logical_not(same_block(b)) & below, a, 0.0)
        p = p - mm(mm(p, off_diag), p)
        b *= 2
    resid = eye - p - _bmm_split(a, p)
    return p + mm(p, resid)


def _dn_prep_kernel(q_ref, k_ref, v_ref, ba_ref, bat_ref, prow_ref, pcol_ref,
                    u_ref, w_ref, qg_ref, kd_ref, intra_ref, eg_ref, *, heads, chunks):
    c_len = DN_CHUNK

    ba = ba_ref[...]
    beta_col = _sigmoid(ba)
    g_col = -jnp.exp(prow_ref[0:1, :]) * _softplus(ba + prow_ref[1:2, :])
    row_in_chunk = lax.broadcasted_iota(jnp.int32, g_col.shape, 0) % c_len
    shift = 1
    while shift < c_len:
        g_col = g_col + jnp.where(row_in_chunk >= shift, pltpu.roll(g_col, shift, 0), 0.0)
        shift *= 2
    bat = bat_ref[...]
    beta_row = _sigmoid(bat)
    g_row = -jnp.exp(pcol_ref[:, 0:1])[None] * _softplus(bat + pcol_ref[:, 1:2][None])
    lane_in_chunk = lax.broadcasted_iota(jnp.int32, g_row.shape, 2)
    shift = 1
    while shift < c_len:
        g_row = g_row + jnp.where(lane_in_chunk >= shift, pltpu.roll(g_row, shift, 2), 0.0)
        shift *= 2
    eg_ref[...] = jnp.exp(g_row)

    ri = lax.broadcasted_iota(jnp.int32, (c_len, c_len), 0)
    ci = lax.broadcasted_iota(jnp.int32, (c_len, c_len), 1)
    lower = ri >= ci
    strict = ri > ci

    units = [(c, h) for c in range(chunks) for h in range(heads)]

    def rows_of(c):
        return slice(c * c_len, (c + 1) * c_len)

    def per_unit(x_ref):
        return jnp.stack([x_ref[rows_of(c), h * HEAD_DIM:(h + 1) * HEAD_DIM] for c, h in units])

    q = per_unit(q_ref).astype(F32) * HEAD_DIM ** -0.5
    k16 = per_unit(k_ref)
    k = k16.astype(F32)
    v16 = per_unit(v_ref)
    gcol = jnp.stack([g_col[rows_of(c), heads + h:heads + h + 1] for c, h in units])
    bcol = jnp.stack([beta_col[rows_of(c), h:h + 1] for c, h in units])
    grow = jnp.stack([g_row[c, heads + h:heads + h + 1, :] for c, h in units])
    brow = jnp.stack([beta_row[c, h:h + 1, :] for c, h in units])
    decay = jnp.exp(jnp.where(lower, gcol - grow, -jnp.inf))
    q16 = q.astype(BF16)
    qk = _bdot(jnp.concatenate([q16, k16], axis=1), k16, ((2,), (2,)))
    intra = qk[:, :c_len] * decay
    a = jnp.where(strict, qk[:, c_len:] * decay * bcol, 0.0)
    t_inv = _unit_lower_inverse_batched(a)
    u = _bmm((t_inv * brow).astype(BF16), v16)
    w = _bmm((t_inv * (brow * jnp.exp(grow))).astype(BF16), k16)
    g_last = grow[:, :, c_len - 1:c_len]
    qg = q * jnp.exp(gcol)
    kd = k * jnp.exp(g_last - gcol)
    for c in range(chunks):
        rs, sl = rows_of(c), slice(c * heads, (c + 1) * heads)
        u_ref[:, rs, :] = u[sl]
        w_ref[:, rs, :] = w[sl].astype(w_ref.dtype)
        qg_ref[:, rs, :] = qg[sl].astype(qg_ref.dtype)
        kd_ref[:, rs, :] = kd[sl].astype(kd_ref.dtype)
        intra_ref[:, rs, :] = intra[sl].astype(intra_ref.dtype)


def _dn_scan_kernel(egl_ref, u_ref, w_ref, qg_ref, kd_ref, intra_ref, z_ref, gn_ref, o_ref,
                    state_ref, *, heads, chunks):
    tb = pl.program_id(0)
    c_len = DN_CHUNK

    @pl.when(tb == 0)
    def _():
        state_ref[...] = jnp.zeros_like(state_ref)

    gn = gn_ref[...]
    for c in range(chunks):
        rs = slice(c * c_len, (c + 1) * c_len)
        s = state_ref[...]
        lhs = jnp.concatenate([w_ref[:, rs, :], qg_ref[:, rs, :]], axis=1)
        r = _bmm(lhs, s.astype(BF16))
        v_new = u_ref[:, rs, :] - r[:, :c_len]
        vn16 = v_new.astype(BF16)
        o = r[:, c_len:] + _bmm(intra_ref[:, rs, :], vn16)
        upd = _bdot(kd_ref[:, rs, :], vn16, ((1,), (1,)))
        base = (tb * chunks + c) * heads
        for h in range(heads):
            state_ref[h] = s[h] * egl_ref[base + h] + upd[h]
        o = o * lax.rsqrt(jnp.mean(o * o, axis=-1, keepdims=True) + EPS) * gn
        for h in range(heads):
            cs = slice(h * HEAD_DIM, (h + 1) * HEAD_DIM)
            zz = z_ref[rs, cs].astype(F32)
            o_ref[rs, cs] = (o[h] * (zz * _sigmoid(zz))).astype(o_ref.dtype)


def gated_deltanet2(qk, v, ba, a_log, dt_bias, z, dn_norm, *, prep_chunks=4, scan_chunks=8):
    s_len, width = v.shape
    heads = width // HEAD_DIM
    c_len = DN_CHUNK
    n_chunks = s_len // c_len
    bat = ba[:, :2 * heads].reshape(n_chunks, c_len, 2 * heads).transpose(0, 2, 1)
    prow = jnp.zeros((2, HEAD_DIM), F32)
    prow = prow.at[0, heads:2 * heads].set(a_log.astype(F32))
    prow = prow.at[1, heads:2 * heads].set(dt_bias.astype(F32))
    pcol = jnp.zeros((2 * heads, 2), F32)
    pcol = pcol.at[heads:, 0].set(a_log.astype(F32)).at[heads:, 1].set(dt_bias.astype(F32))

    rows = prep_chunks * c_len

    def main_spec(i):
        return pl.BlockSpec((rows, width), lambda t: (t, i))

    def prep_out(last):
        return pl.BlockSpec((heads, rows, last), lambda t: (0, t, 0))

    u, w, qg, kd, intra, eg = pl.pallas_call(
        functools.partial(_dn_prep_kernel, heads=heads, chunks=prep_chunks),
        grid=(s_len // rows,),
        in_specs=[main_spec(0), main_spec(1), main_spec(0),
                  pl.BlockSpec((rows, HEAD_DIM), lambda t: (t, 0)),
                  pl.BlockSpec((prep_chunks, 2 * heads, c_len), lambda t: (t, 0, 0)),
                  pl.BlockSpec((2, HEAD_DIM), lambda t: (0, 0)),
                  pl.BlockSpec((2 * heads, 2), lambda t: (0, 0))],
        out_specs=[prep_out(HEAD_DIM), prep_out(HEAD_DIM), prep_out(HEAD_DIM),
                   prep_out(HEAD_DIM), prep_out(c_len),
                   pl.BlockSpec((prep_chunks, 2 * heads, c_len), lambda t: (t, 0, 0))],
        out_shape=[jax.ShapeDtypeStruct((heads, s_len, HEAD_DIM), F32),
                   jax.ShapeDtypeStruct((heads, s_len, HEAD_DIM), BF16),
                   jax.ShapeDtypeStruct((heads, s_len, HEAD_DIM), BF16),
                   jax.ShapeDtypeStruct((heads, s_len, HEAD_DIM), BF16),
                   jax.ShapeDtypeStruct((heads, s_len, c_len), BF16),
                   jax.ShapeDtypeStruct((n_chunks, 2 * heads, c_len), F32)],
        compiler_params=_params(1),
        name="deltanet_prep",
    )(qk, qk, v, ba, bat, prow, pcol)

    egl = eg[:, heads:, c_len - 1].reshape(n_chunks * heads)

    scan_rows = scan_chunks * c_len

    def head_major(last):
        return pl.BlockSpec((heads, scan_rows, last), lambda t: (0, t, 0))

    nat = pl.BlockSpec((scan_rows, width), lambda t: (t, 0))
    return pl.pallas_call(
        functools.partial(_dn_scan_kernel, heads=heads, chunks=scan_chunks),
        grid=(s_len // scan_rows,),
        in_specs=[pl.BlockSpec(memory_space=pltpu.SMEM),
                  head_major(HEAD_DIM), head_major(HEAD_DIM), head_major(HEAD_DIM),
                  head_major(HEAD_DIM), head_major(c_len), nat,
                  pl.BlockSpec((1, HEAD_DIM), lambda t: (0, 0))],
        out_specs=nat,
        out_shape=jax.ShapeDtypeStruct((s_len, width), BF16),
        scratch_shapes=[pltpu.VMEM((heads, HEAD_DIM, HEAD_DIM), F32)],
        compiler_params=_params(1),
        name="deltanet_scan",
    )(egl, u, w, qg, kd, intra, z, dn_norm.reshape(1, HEAD_DIM).astype(F32))


def _proj_norm_kernel(p_ref, w_ref, g_ref, o_ref, wcast_ref):
    @pl.when(pl.program_id(0) == 0)
    def _():
        wcast_ref[...] = w_ref[...].astype(BF16)

    y = jnp.dot(p_ref[...].astype(BF16), wcast_ref[...], preferred_element_type=F32)
    ms = jnp.mean(y * y, axis=-1, keepdims=True)
    o_ref[...] = (y * lax.rsqrt(ms + EPS) * g_ref[...]).astype(o_ref.dtype)


def proj_norm(p, w, g, out_dtype, *, tm=512):
    m, kdim = p.shape
    n = w.shape[1]
    return pl.pallas_call(
        _proj_norm_kernel,
        grid=(m // tm,),
        in_specs=[pl.BlockSpec((tm, kdim), lambda i: (i, 0)),
                  pl.BlockSpec((kdim, n), lambda i: (0, 0)),
                  pl.BlockSpec((1, n), lambda i: (0, 0))],
        out_specs=pl.BlockSpec((tm, n), lambda i: (i, 0)),
        out_shape=jax.ShapeDtypeStruct((m, n), out_dtype),
        scratch_shapes=[pltpu.VMEM((kdim, n), BF16)],
        compiler_params=_params(1),
        name="proj_norm",
    )(p, w, g.reshape(1, n).astype(F32))


def _layer(x, p, w_in, conv_w, a_log, dt_bias, dn_norm, w_attn_up, w_dn_up, w_out, w_mlp_up,
           w_mlp_down, w_ple_gate, w_ple_proj, norm_mix, norm_mlp, norm_ple, ple_post_norm):
    s_len, d_model = x.shape
    attn_w = ATTN_HEADS * HEAD_DIM
    dn_w = DN_HEADS * HEAD_DIM
    n_groups = len(ATTN_DILATIONS)
    c_qkv_b = 3 * n_groups * attn_w
    c_z = c_qkv_b + 3 * dn_w
    c_ba = c_z + dn_w
    c_gate = c_ba + 2 * DN_HEADS
    mm = matmul_ws

    w_in_t = w_in.T
    h = rmsnorm(x, norm_mix, BF16)
    max_dil = max(ATTN_DILATIONS)
    z_panels = dn_w // attn_w
    z_qkv_g0 = mm(h, w_in_t, b_is_nk=True, n=dn_w + 3 * attn_w, tn=attn_w, out_dtype=BF16,
                  panel_col=lambda j: jnp.where(j < z_panels, c_z + j * attn_w,
                                                (j - z_panels) * (n_groups * attn_w)),
                  name="proj_z_attn_g0")
    qkv_g12 = mm(h, w_in_t, b_is_nk=True, n=6 * attn_w, tn=attn_w, out_dtype=BF16,
                 epilogue="plain_by_residue", residues=max_dil,
                 panel_col=lambda j: (j // 2) * (n_groups * attn_w) + (1 + j % 2) * attn_w,
                 name="proj_attn_g12")
    qk_b = mm(h, w_in_t, b_is_nk=True, col0=c_qkv_b, n=2 * dn_w, tn=1024, out_dtype=BF16,
              epilogue="conv_silu_l2norm", extra=((conv_w, 0),), name="proj_dn_qk")
    v_b = mm(h, w_in_t, b_is_nk=True, col0=c_qkv_b + 2 * dn_w, n=dn_w, tn=1024, out_dtype=BF16,
             epilogue="conv_silu", extra=((conv_w, 2 * dn_w),), name="proj_dn_v")
    ba = mm(h, w_in_t, b_is_nk=True, col0=c_ba, n=LANES, tm=2048, tn=LANES, out_dtype=F32,
            name="proj_ba")
    gates = mm(h, w_in_t, b_is_nk=True, col0=c_gate, n=2 * d_model, tn=1024, out_dtype=BF16,
               epilogue="sigmoid", name="proj_gates")

    g0 = z_qkv_g0.reshape(1, 1, s_len, dn_w + 3 * attn_w)
    o0, lse0 = banded_attention(g0, g0, g0,
                                col_blocks=(z_panels, z_panels + 1, z_panels + 2))
    per = s_len // max_dil
    g1 = qkv_g12.reshape(4, 4, per, 6 * attn_w)
    o1, lse1 = banded_attention(g1, g1, g1, col_blocks=(0, 2, 4))
    g2 = qkv_g12.reshape(1, max_dil, per, 6 * attn_w)
    o2, lse2 = banded_attention(g2, g2, g2, col_blocks=(1, 3, 5))

    def natural(t):
        return t.transpose(2, 0, 1, 3).reshape(s_len, t.shape[-1])

    o_a = attn_combine([o0.reshape(s_len, attn_w), natural(o1), natural(o2)],
                       [lse0.reshape(s_len, HEAD_DIM), natural(lse1), natural(lse2)])

    o_b = gated_deltanet2(qk_b, v_b, ba, a_log, dt_bias, z_qkv_g0, dn_norm)

    t_a = mm(o_a, w_attn_up, tm=2048, tn=1024, out_dtype=BF16, epilogue="mul",
             extra=((gates, 0),), name="attn_up")
    merged = mm(o_b, w_dn_up, tm=2048, tn=1024, out_dtype=BF16, epilogue="mul_add",
                extra=((gates, d_model), (t_a, 0)), name="dn_up")
    x = mm(merged, w_out, tm=1024, tn=1024, out_dtype=F32, epilogue="residual", extra=((x, 0),),
           name="out_proj")

    h = rmsnorm(x, norm_mlp, BF16)
    u = mm(h, w_mlp_up, tn=1024, out_dtype=BF16, epilogue="relu2", name="mlp_up")
    x = mm(u, w_mlp_down, out_dtype=F32, epilogue="residual", extra=((x, 0),), tk=4096,
           name="mlp_down")

    h = rmsnorm(x, norm_ple, BF16)
    pp = proj_norm(p, w_ple_proj, ple_post_norm, BF16)
    x = mm(h, w_ple_gate, tm=512, tn=1024, out_dtype=F32, epilogue="ple",
           extra=((x, 0), (pp, 0)), name="ple_gate")
    return x


def kernel(x, p, w_in, conv_w, dn_a_log, dn_dt_bias, dn_norm, w_attn_up, w_dn_up, w_out,
           w_mlp_up, w_mlp_down, w_ple_gate, w_ple_proj, norm_mix, norm_mlp, norm_ple,
           ple_post_norm, final_norm):
    b, s_len, d_model = x.shape
    assert b == 1
    depth = w_in.shape[0]
    xs = x.reshape(s_len, d_model)
    for i in range(depth):
        xs = _layer(xs, p[i, 0], w_in[i], conv_w[i], dn_a_log[i], dn_dt_bias[i], dn_norm[i],
                    w_attn_up[i], w_dn_up[i], w_out[i], w_mlp_up[i], w_mlp_down[i],
                    w_ple_gate[i], w_ple_proj[i], norm_mix[i], norm_mlp[i], norm_ple[i],
                    ple_post_norm[i])
    out = rmsnorm(xs, final_norm, x.dtype)
    return out.reshape(b, s_len, d_model)
```
